```python
import math
import jax, jax.numpy as jnp
from jax import lax
import numpy as np

D_MODEL = 1024
BATCH = 4
SEQ = 8192
DEPTH = 1

CHUNK = 64
Q_BLOCK = 128

GLA_HEADS = 4
GLA_DK = 128
GLA_DV = 256
GLA_LOWRANK = 16
GLA_GATE_TEMP = 16.0

MLA_HEADS = 8
MLA_Q_LORA = 384
MLA_KV_LORA = 256
MLA_NOPE = 128
MLA_ROPE = 64
MLA_V = 128
ROPE_THETA = 10000.0

D_FF = ((8 * D_MODEL // 3 + 255) // 256) * 256

NORM_EPS = 1e-6

IN_SPLITS = (
    GLA_HEADS * GLA_DK,
    GLA_HEADS * GLA_DK,
    GLA_HEADS * GLA_DV,
    GLA_HEADS * GLA_DV,
    GLA_LOWRANK,
    MLA_Q_LORA,
    MLA_KV_LORA,
    MLA_ROPE,
)
D_IN = sum(IN_SPLITS)
GLA_WIDTH = GLA_HEADS * GLA_DV
MLA_WIDTH = MLA_HEADS * MLA_V

kernel_name = "hybrid_gla_mla_gated_sandwich_block"


def rms_norm(x, g):
    xf = x.astype(jnp.float32)
    y = xf * lax.rsqrt(jnp.mean(xf * xf, axis=-1, keepdims=True) + NORM_EPS)
    return (y * g.astype(jnp.float32)).astype(x.dtype)


def rope_tables(positions):
    inv_freq = 1.0 / (ROPE_THETA ** (jnp.arange(0, MLA_ROPE, 2, dtype=jnp.float32) / MLA_ROPE))
    ang = positions.astype(jnp.float32)[..., None] * inv_freq
    return jnp.cos(ang), jnp.sin(ang)


def apply_rope(x, cos, sin):
    half = x.shape[-1] // 2
    x1 = x[..., :half].astype(jnp.float32)
    x2 = x[..., half:].astype(jnp.float32)
    out = jnp.concatenate([x1 * cos - x2 * sin, x2 * cos + x1 * sin], axis=-1)
    return out.astype(x.dtype)


def gla_branch(h_q, h_k, h_v, h_g, h_a, w_a2, b_a2, gla_norm):
    B, S, _ = h_q.shape
    n_chunks = S // CHUNK
    f32 = jnp.float32
    q = h_q.reshape(B, S, GLA_HEADS, GLA_DK).astype(f32) * (GLA_DK ** -0.5)
    k = h_k.reshape(B, S, GLA_HEADS, GLA_DK).astype(f32)
    v = h_v.reshape(B, S, GLA_HEADS, GLA_DV).astype(f32)
    log_a = jax.nn.log_sigmoid((h_a @ w_a2 + b_a2).astype(f32)) / GLA_GATE_TEMP
    log_a = log_a.reshape(B, S, GLA_HEADS, GLA_DK)

    def to_chunks(t):
        return t.reshape(B, n_chunks, CHUNK, GLA_HEADS, t.shape[-1]).transpose(1, 0, 3, 2, 4)

    q, k, v, log_a = to_chunks(q), to_chunks(k), to_chunks(v), to_chunks(log_a)
    b = jnp.cumsum(log_a, axis=3)
    b_end = b[:, :, :, -1:, :]
    k_dec = k * jnp.exp(b_end - b)
    chunk_decay = jnp.exp(b_end[:, :, :, 0, :])

    def step(state, inp):
        qc, kc, vc, dc = inp
        state = dc[..., None] * state + jnp.einsum('bhck,bhcv->bhkv', kc, vc)
        out = jnp.einsum('bhck,bhkv->bhcv', qc, state)
        return state, out

    state0 = jnp.zeros((B, GLA_HEADS, GLA_DK, GLA_DV), f32)
    _, o = lax.scan(step, state0, (q, k_dec, v, chunk_decay))
    o = o.transpose(1, 0, 3, 2, 4).reshape(B, S, GLA_HEADS, GLA_DV)
    o = rms_norm(o, gla_norm)
    g = h_g.reshape(B, S, GLA_HEADS, GLA_DV).astype(f32)
    o = o * jax.nn.silu(g)
    return o.reshape(B, S, GLA_WIDTH).astype(h_q.dtype)


def mla_branch(c_q, c_kv, k_pe, cos, sin, q_norm, w_uq, kv_norm, w_ukv):
    B, S, _ = c_q.shape
    q = (rms_norm(c_q, q_norm) @ w_uq).reshape(B, S, MLA_HEADS, MLA_NOPE + MLA_ROPE)
    q_nope = q[..., :MLA_NOPE]
    q_pe = apply_rope(q[..., MLA_NOPE:], cos[:, :, None, :], sin[:, :, None, :])
    kv = (rms_norm(c_kv, kv_norm) @ w_ukv).reshape(B, S, MLA_HEADS, MLA_NOPE + MLA_V)
    k_nope = kv[..., :MLA_NOPE]
    v = kv[..., MLA_NOPE:]
    k_pe = apply_rope(k_pe, cos, sin)
    scale = (MLA_NOPE + MLA_ROPE) ** -0.5
    n_blocks = S // Q_BLOCK

    def blocks(t):
        return t.reshape(B, n_blocks, Q_BLOCK, MLA_HEADS, t.shape[-1]).transpose(1, 0, 2, 3, 4)

    key_chunk = jnp.arange(S) // CHUNK

    def attend(args):
        qn_b, qp_b, blk = args
        s = (jnp.einsum('bqhd,bkhd->bhqk', qn_b, k_nope)
             + jnp.einsum('bqhr,bkr->bhqk', qp_b, k_pe)).astype(jnp.float32) * scale
        q_chunk = (blk * Q_BLOCK + jnp.arange(Q_BLOCK)) // CHUNK
        mask = key_chunk[None, :] <= q_chunk[:, None]
        s = jnp.where(mask[None, None], s, -jnp.inf)
        p = jax.nn.softmax(s, axis=-1).astype(v.dtype)
        return jnp.einsum('bhqk,bkhd->bqhd', p, v)

    o = lax.map(attend, (blocks(q_nope), blocks(q_pe), jnp.arange(n_blocks)))
    return o.transpose(1, 0, 2, 3, 4).reshape(B, S, MLA_WIDTH)


def setup_inputs(seed: int = 0) -> dict:
    key = jax.random.key(seed)
    ks = jax.random.split(key, 24)
    f32 = jnp.float32

    def w(k, shape, fan_in):
        return jax.random.normal(k, shape, f32) * (fan_in ** -0.5)

    def gain(k, dim):
        return 1.0 + 0.02 * jax.random.normal(k, (DEPTH, dim), f32)

    x = jax.random.normal(ks[0], (BATCH, SEQ, D_MODEL), f32)
    offsets = jax.random.randint(ks[1], (BATCH, 1), 0, 4096, dtype=jnp.int32)
    positions = offsets + jnp.arange(SEQ, dtype=jnp.int32)[None, :]
    return {
        "x": x,
        "positions": positions,
        "pre_mix_norm": gain(ks[2], D_MODEL),
        "w_in": w(ks[3], (DEPTH, D_MODEL, D_IN), D_MODEL),
        "w_a2": w(ks[4], (DEPTH, GLA_LOWRANK, GLA_HEADS * GLA_DK), GLA_LOWRANK),
        "b_a2": 0.1 * jax.random.normal(ks[5], (DEPTH, GLA_HEADS * GLA_DK), f32),
        "gla_norm": gain(ks[6], GLA_DV),
        "w_o_gla": w(ks[7], (DEPTH, GLA_WIDTH, D_MODEL), GLA_WIDTH),
        "q_norm": gain(ks[8], MLA_Q_LORA),
        "w_uq": w(ks[9], (DEPTH, MLA_Q_LORA, MLA_HEADS * (MLA_NOPE + MLA_ROPE)), MLA_Q_LORA),
        "kv_norm": gain(ks[10], MLA_KV_LORA),
        "w_ukv": w(ks[11], (DEPTH, MLA_KV_LORA, MLA_HEADS * (MLA_NOPE + MLA_V)), MLA_KV_LORA),
        "w_o_mla": w(ks[12], (DEPTH, MLA_WIDTH, D_MODEL), MLA_WIDTH),
        "w_gate": w(ks[13], (DEPTH, D_MODEL, 2 * D_MODEL), D_MODEL),
        "b_gate": 0.02 * jax.random.normal(ks[14], (DEPTH, 2 * D_MODEL), f32),
        "w_out": w(ks[15], (DEPTH, D_MODEL, D_MODEL), D_MODEL),
        "post_mix_norm": gain(ks[16], D_MODEL),
        "pre_ffn_norm": gain(ks[17], D_MODEL),
        "w_ffn_gate": w(ks[18], (DEPTH, D_MODEL, D_FF), D_MODEL),
        "w_ffn_up": w(ks[19], (DEPTH, D_MODEL, D_FF), D_MODEL),
        "w_ffn_down": w(ks[20], (DEPTH, D_FF, D_MODEL), D_FF),
        "post_ffn_norm": gain(ks[21], D_MODEL),
    }


def reference(x, positions, pre_mix_norm, w_in, w_a2, b_a2, gla_norm, w_o_gla,
              q_norm, w_uq, kv_norm, w_ukv, w_o_mla, w_gate, b_gate, w_out,
              post_mix_norm, pre_ffn_norm, w_ffn_gate, w_ffn_up, w_ffn_down,
              post_ffn_norm):
    cos, sin = rope_tables(positions)
    split_points = list(np.cumsum(IN_SPLITS)[:-1])
    for l in range(DEPTH):
        h = rms_norm(x, pre_mix_norm[l])
        proj = h @ w_in[l]
        h_q, h_k, h_v, h_g, h_a, c_q, c_kv, k_pe = jnp.split(proj, split_points, axis=-1)
        y_a = gla_branch(h_q, h_k, h_v, h_g, h_a, w_a2[l], b_a2[l], gla_norm[l]) @ w_o_gla[l]
        y_b = mla_branch(c_q, c_kv, k_pe, cos, sin, q_norm[l], w_uq[l], kv_norm[l], w_ukv[l]) @ w_o_mla[l]
        gates = jax.nn.sigmoid(h @ w_gate[l] + b_gate[l])
        g_a, g_b = jnp.split(gates, 2, axis=-1)
        mixed = (g_a * y_a + g_b * y_b) @ w_out[l]
        x = x + rms_norm(mixed, post_mix_norm[l])
        h = rms_norm(x, pre_ffn_norm[l])
        f = (jax.nn.silu(h @ w_ffn_gate[l]) * (h @ w_ffn_up[l])) @ w_ffn_down[l]
        x = x + rms_norm(f, post_ffn_norm[l])
    return x
```

```python
import functools

import jax
import jax.numpy as jnp
from jax import lax
from jax.experimental import pallas as pl
from jax.experimental.pallas import tpu as pltpu

CHUNK = 64
GLA_HEADS = 4
GLA_DK = 128
GLA_DV = 256
GLA_LOWRANK = 16
GLA_GATE_TEMP = 16.0
MLA_HEADS = 8
MLA_Q_LORA = 384
MLA_KV_LORA = 256
MLA_NOPE = 128
MLA_ROPE = 64
MLA_V = 128
MLA_QK = MLA_NOPE + MLA_ROPE
ROPE_THETA = 10000.0
NORM_EPS = 1e-6

GLA_KW = GLA_HEADS * GLA_DK
GLA_VW = GLA_HEADS * GLA_DV
MLA_W = MLA_HEADS * MLA_V
SMALL_W = 768
KPE_OFF = MLA_Q_LORA + MLA_KV_LORA
HA_OFF = KPE_OFF + MLA_ROPE
LANES = 128

VMEM_LIMIT_BYTES = 56 * 1024 * 1024

BF16 = jnp.bfloat16
F32 = jnp.float32

_NT = (((1,), (1,)), ((), ()))


def _dot(a, b):
    return jnp.dot(a, b, preferred_element_type=F32)


def _dot_nt(a, b):
    return lax.dot_general(a, b, _NT, preferred_element_type=F32)


def _rms(x, gain):
    ms = jnp.mean(x * x, axis=-1, keepdims=True)
    return x * lax.rsqrt(ms + NORM_EPS) * gain


def _sigmoid(z):
    return 1.0 / (1.0 + jnp.exp(-z))


def _const_spec(shape):
    nd = len(shape)
    return pl.BlockSpec(shape, lambda *_: (0,) * nd, pipeline_mode=pl.Buffered(1))


def _params(sem):
    return pltpu.CompilerParams(dimension_semantics=sem, vmem_limit_bytes=VMEM_LIMIT_BYTES)


def _in_proj_kernel(x_ref, gn_ref, wq_ref, wkT_ref, wv_ref, wg_ref, ws_ref, wa2T_ref, ba2_ref,
                    q_ref, kT_ref, v_ref, g_ref, laT_ref, small_ref):
    h = _rms(x_ref[...], gn_ref[...]).astype(BF16)
    q_ref[...] = (_dot(h, wq_ref[...]) * (GLA_DK ** -0.5)).astype(BF16)
    kT_ref[...] = _dot_nt(wkT_ref[...], h).astype(BF16)
    v_ref[...] = _dot(h, wv_ref[...]).astype(BF16)
    g_ref[...] = _dot(h, wg_ref[...]).astype(BF16)
    small = _dot(h, ws_ref[...])
    small_ref[...] = small
    last = small[:, KPE_OFF:].astype(BF16)
    zT = _dot_nt(wa2T_ref[...], last) + ba2_ref[...]
    log_sig = jnp.minimum(zT, 0.0) - jnp.log(1.0 + jnp.exp(-jnp.abs(zT)))
    laT_ref[...] = log_sig * (1.0 / GLA_GATE_TEMP)


def _in_proj(x2, gn, wq, wkT, wv, wg, ws, wa2T, ba2, tm):
    T, D = x2.shape
    grid = (T // tm,)
    row = lambda i: (i, 0)
    col = lambda i: (0, i)
    return pl.pallas_call(
        _in_proj_kernel,
        grid=grid,
        in_specs=[
            pl.BlockSpec((tm, D), row),
            _const_spec(gn.shape), _const_spec(wq.shape), _const_spec(wkT.shape),
            _const_spec(wv.shape), _const_spec(wg.shape), _const_spec(ws.shape),
            _const_spec(wa2T.shape), _const_spec(ba2.shape),
        ],
        out_specs=[
            pl.BlockSpec((tm, GLA_KW), row),
            pl.BlockSpec((GLA_KW, tm), col),
            pl.BlockSpec((tm, GLA_VW), row),
            pl.BlockSpec((tm, GLA_VW), row),
            pl.BlockSpec((GLA_KW, tm), col),
            pl.BlockSpec((tm, SMALL_W), row),
        ],
        out_shape=[
            jax.ShapeDtypeStruct((T, GLA_KW), BF16),
            jax.ShapeDtypeStruct((GLA_KW, T), BF16),
            jax.ShapeDtypeStruct((T, GLA_VW), BF16),
            jax.ShapeDtypeStruct((T, GLA_VW), BF16),
            jax.ShapeDtypeStruct((GLA_KW, T), F32),
            jax.ShapeDtypeStruct((T, SMALL_W), F32),
        ],
        compiler_params=_params(("parallel",)),
        name="in_proj",
    )(x2, gn, wq, wkT, wv, wg, ws, wa2T, ba2)


PAIR = 2 * CHUNK


def _gla_kernel(q_ref, kT_ref, laT_ref, v_ref, g_ref, gn_ref, o_ref, state_ref, *, ts):
    @pl.when(pl.program_id(1) == 0)
    def _():
        state_ref[...] = jnp.zeros_like(state_ref)

    row = lax.broadcasted_iota(jnp.int32, (PAIR, PAIR), 0)
    col = lax.broadcasted_iota(jnp.int32, (PAIR, PAIR), 1)
    later = ((row // CHUNK == col // CHUNK) & (row > col)).astype(BF16)
    lane = lax.broadcasted_iota(jnp.int32, (GLA_DK, PAIR), 1)
    first = lane < CHUNK
    gn = gn_ref[...]

    for h in range(GLA_HEADS):
        ks = slice(h * GLA_DK, (h + 1) * GLA_DK)
        vs = slice(h * GLA_DV, (h + 1) * GLA_DV)
        st = state_ref[h]
        for p in range(ts // PAIR):
            tok = slice(p * PAIR, (p + 1) * PAIR)
            la = laT_ref[ks, tok]
            la_hi = la.astype(BF16)
            la_lo = (la - la_hi.astype(F32)).astype(BF16)
            to_end = _dot(la_hi, later) + _dot(la_lo, later)
            kd = kT_ref[ks, tok].astype(F32) * jnp.exp(to_end)
            kd0 = jnp.where(first, kd, 0.0).astype(BF16)
            kd1 = jnp.where(first, 0.0, kd).astype(BF16)
            dec0 = jnp.exp(jnp.sum(jnp.where(first, la, 0.0), axis=1, keepdims=True))
            dec1 = jnp.exp(jnp.sum(jnp.where(first, 0.0, la), axis=1, keepdims=True))
            v = v_ref[tok, vs]
            q = q_ref[tok, ks]
            st = dec0 * st + _dot(kd0, v)
            o0 = _dot(q[:CHUNK], st.astype(BF16))
            st = dec1 * st + _dot(kd1, v)
            o1 = _dot(q[CHUNK:], st.astype(BF16))
            o = jnp.concatenate([o0, o1], axis=0)
            o = _rms(o, gn)
            g = g_ref[tok, vs].astype(F32)
            o_ref[tok, vs] = (o * (g * _sigmoid(g))).astype(BF16)
        state_ref[h] = st


def _gla(q, kT, laT, v, g, gn, B, S, ts):
    T = B * S
    nt = S // ts
    row = lambda b, t: (b * nt + t, 0)
    col = lambda b, t: (0, b * nt + t)
    return pl.pallas_call(
        functools.partial(_gla_kernel, ts=ts),
        grid=(B, nt),
        in_specs=[
            pl.BlockSpec((ts, GLA_KW), row),
            pl.BlockSpec((GLA_KW, ts), col),
            pl.BlockSpec((GLA_KW, ts), col),
            pl.BlockSpec((ts, GLA_VW), row),
            pl.BlockSpec((ts, GLA_VW), row),
            _const_spec(gn.shape),
        ],
        out_specs=pl.BlockSpec((ts, GLA_VW), row),
        out_shape=jax.ShapeDtypeStruct((T, GLA_VW), BF16),
        scratch_shapes=[pltpu.VMEM((GLA_HEADS, GLA_DK, GLA_DV), F32)],
        compiler_params=_params(("parallel", "arbitrary")),
        name="gla",
    )(q, kT, laT, v, g, gn)


def _mla_prep_kernel(small_ref, pos_ref, invf_ref, qn_ref, wuqT_ref, kvn_ref, wuk_ref, wuvT_ref,
                     qT_ref, k_ref, vT_ref, *, tm):
    sm = small_ref[...]
    cq = _rms(sm[:, :MLA_Q_LORA], qn_ref[...]).astype(BF16)
    ckv = _rms(sm[:, MLA_Q_LORA:KPE_OFF], kvn_ref[...]).astype(BF16)
    last = sm[:, KPE_OFF:]

    half = MLA_ROPE // 2
    ang = invf_ref[...] * pos_ref[...].astype(F32)
    cos = jnp.cos(ang)
    sin = jnp.sin(ang)

    scale = MLA_QK ** -0.5
    qT = _dot_nt(wuqT_ref[...], cq)
    for h in range(MLA_HEADS):
        b0 = h * MLA_QK
        qT_ref[b0:b0 + MLA_NOPE, :] = (qT[b0:b0 + MLA_NOPE] * scale).astype(BF16)
        x1 = qT[b0 + MLA_NOPE:b0 + MLA_NOPE + half]
        x2 = qT[b0 + MLA_NOPE + half:b0 + MLA_QK]
        qT_ref[b0 + MLA_NOPE:b0 + MLA_NOPE + half, :] = ((x1 * cos - x2 * sin) * scale).astype(BF16)
        qT_ref[b0 + MLA_NOPE + half:b0 + MLA_QK, :] = ((x2 * cos + x1 * sin) * scale).astype(BF16)

    zeros = jnp.zeros((LANES - MLA_ROPE, tm), F32)
    c_tab = jnp.concatenate([cos, cos, zeros], axis=0).T
    s_tab = jnp.concatenate([-sin, sin, zeros], axis=0).T
    lane = lax.broadcasted_iota(jnp.int32, last.shape, 1)
    swapped = jnp.where(lane < half, pltpu.roll(last, LANES - half, axis=1), pltpu.roll(last, half, axis=1))
    kpe = (last * c_tab + swapped * s_tab)[:, :MLA_ROPE].astype(BF16)

    k_nope = _dot(ckv, wuk_ref[...]).astype(BF16)
    for h in range(MLA_HEADS):
        k_ref[h, :, :MLA_NOPE] = k_nope[:, h * MLA_NOPE:(h + 1) * MLA_NOPE]
        k_ref[h, :, MLA_NOPE:] = kpe
    vT = _dot_nt(wuvT_ref[...], ckv).astype(BF16)
    for h in range(MLA_HEADS):
        vT_ref[h] = vT[h * MLA_V:(h + 1) * MLA_V]


def _mla_prep(small, pos3, invf, qn, wuqT, kvn, wuk, wuvT, B, S, tm):
    nt = S // tm
    return pl.pallas_call(
        functools.partial(_mla_prep_kernel, tm=tm),
        grid=(B, nt),
        in_specs=[
            pl.BlockSpec((tm, SMALL_W), lambda b, t: (b * nt + t, 0)),
            pl.BlockSpec((None, 1, tm), lambda b, t: (b, 0, t)),
            _const_spec(invf.shape), _const_spec(qn.shape), _const_spec(wuqT.shape),
            _const_spec(kvn.shape), _const_spec(wuk.shape), _const_spec(wuvT.shape),
        ],
        out_specs=[
            pl.BlockSpec((None, MLA_HEADS * MLA_QK, tm), lambda b, t: (b, 0, t)),
            pl.BlockSpec((None, MLA_HEADS, None, tm, MLA_QK), lambda b, t: (b, 0, t, 0, 0)),
            pl.BlockSpec((None, MLA_HEADS, None, MLA_V, tm), lambda b, t: (b, 0, t, 0, 0)),
        ],
        out_shape=[
            jax.ShapeDtypeStruct((B, MLA_HEADS * MLA_QK, S), BF16),
            jax.ShapeDtypeStruct((B, MLA_HEADS, nt, tm, MLA_QK), BF16),
            jax.ShapeDtypeStruct((B, MLA_HEADS, nt, MLA_V, tm), BF16),
        ],
        compiler_params=_params(("parallel", "parallel")),
        name="mla_prep",
    )(small, pos3, invf, qn, wuqT, kvn, wuk, wuvT)


def _mla_attn_kernel(qT_ref, k_ref, vT_ref, o_ref, *, tq):
    i = pl.program_id(2)
    qT = qT_ref[...]

    def step(j, carry, masked):
        m, l, acc = carry
        sT = _dot(k_ref[j], qT)
        if masked:
            kc = lax.broadcasted_iota(jnp.int32, sT.shape, 0) // CHUNK
            qc = lax.broadcasted_iota(jnp.int32, sT.shape, 1) // CHUNK
            sT = jnp.where(kc <= qc, sT, -jnp.inf)
        m_new = jnp.maximum(m, jnp.max(sT, axis=0, keepdims=True))
        alpha = jnp.exp(m - m_new)
        p = jnp.exp(sT - m_new)
        l = alpha * l + jnp.sum(p, axis=0, keepdims=True)
        acc = alpha * acc + _dot(vT_ref[j], p.astype(BF16))
        return m_new, l, acc

    init = (jnp.full((1, tq), -jnp.inf, F32), jnp.zeros((1, tq), F32), jnp.zeros((MLA_V, tq), F32))
    carry = lax.fori_loop(0, i, lambda j, c: step(j, c, False), init)
    _, l, acc = step(i, carry, True)
    o_ref[...] = (acc / l).T.astype(o_ref.dtype)


def _mla_attn(qT, k5, vT5, B, S, tq):
    nq = S // tq
    return pl.pallas_call(
        functools.partial(_mla_attn_kernel, tq=tq),
        grid=(B, MLA_HEADS, nq),
        in_specs=[
            pl.BlockSpec((None, MLA_QK, tq), lambda b, h, i: (b, h, i)),
            pl.BlockSpec((None, None, nq, tq, MLA_QK), lambda b, h, i: (b, h, 0, 0, 0)),
            pl.BlockSpec((None, None, nq, MLA_V, tq), lambda b, h, i: (b, h, 0, 0, 0)),
        ],
        out_specs=pl.BlockSpec((None, tq, MLA_V), lambda b, h, i: (b, i, h)),
        out_shape=jax.ShapeDtypeStruct((B, S, MLA_W), BF16),
        compiler_params=_params(("parallel", "parallel", "arbitrary")),
        name="mla_attn",
    )(qT, k5, vT5)


def _mix_kernel(x_ref, a_ref, b_ref, gn_ref, wgate_ref, bgate_ref, woa_ref, wob_ref, wout_ref, pn_ref, o_ref):
    x = x_ref[...]
    d = x.shape[-1]
    h = _rms(x, gn_ref[...]).astype(BF16)
    gates = _sigmoid(_dot(h, wgate_ref[...]) + bgate_ref[...])
    y_a = _dot(a_ref[...], woa_ref[...])
    y_b = _dot(b_ref[...], wob_ref[...])
    mixed = (gates[:, :d] * y_a + gates[:, d:] * y_b).astype(BF16)
    o_ref[...] = x + _rms(_dot(mixed, wout_ref[...]), pn_ref[...])


def _mix(x2, a, b, gn, wgate, bgate, woa, wob, wout, pn, tm):
    T, D = x2.shape
    row = lambda i: (i, 0)
    return pl.pallas_call(
        _mix_kernel,
        grid=(T // tm,),
        in_specs=[
            pl.BlockSpec((tm, D), row), pl.BlockSpec((tm, a.shape[1]), row), pl.BlockSpec((tm, b.shape[1]), row),
            _const_spec(gn.shape), _const_spec(wgate.shape), _const_spec(bgate.shape),
            _const_spec(woa.shape), _const_spec(wob.shape), _const_spec(wout.shape), _const_spec(pn.shape),
        ],
        out_specs=pl.BlockSpec((tm, D), row),
        out_shape=jax.ShapeDtypeStruct((T, D), F32),
        compiler_params=_params(("parallel",)),
        name="mix",
    )(x2, a, b, gn, wgate, bgate, woa, wob, wout, pn)


def _ffn_kernel(x_ref, gn_ref, wg_ref, wu_ref, wd_ref, pn_ref, o_ref):
    x = x_ref[...]
    h = _rms(x, gn_ref[...]).astype(BF16)
    gate = _dot(h, wg_ref[...])
    up = _dot(h, wu_ref[...])
    act = (gate * _sigmoid(gate) * up).astype(BF16)
    o_ref[...] = x + _rms(_dot(act, wd_ref[...]), pn_ref[...])


def _ffn(x2, gn, wg, wu, wd, pn, tm):
    T, D = x2.shape
    row = lambda i: (i, 0)
    return pl.pallas_call(
        _ffn_kernel,
        grid=(T // tm,),
        in_specs=[
            pl.BlockSpec((tm, D), row),
            _const_spec(gn.shape), _const_spec(wg.shape), _const_spec(wu.shape),
            _const_spec(wd.shape), _const_spec(pn.shape),
        ],
        out_specs=pl.BlockSpec((tm, D), row),
        out_shape=jax.ShapeDtypeStruct((T, D), F32),
        compiler_params=_params(("parallel",)),
        name="ffn",
    )(x2, gn, wg, wu, wd, pn)


def _tile(n, pref):
    t = min(n, pref)
    assert n % t == 0, (n, t)
    return t


def _layer(x2, pos3, B, S, pre_mix_norm, w_in, w_a2, b_a2, gla_norm, w_o_gla, q_norm, w_uq, kv_norm, w_ukv,
           w_o_mla, w_gate, b_gate, w_out, post_mix_norm, pre_ffn_norm, w_ffn_gate, w_ffn_up, w_ffn_down,
           post_ffn_norm):
    T, D = x2.shape
    bf = lambda w: w.astype(BF16)
    r2 = lambda v: v.reshape(1, -1)

    o = 0
    wq = bf(w_in[:, o:o + GLA_KW]); o += GLA_KW
    wkT = bf(w_in[:, o:o + GLA_KW]).T; o += GLA_KW
    wv = bf(w_in[:, o:o + GLA_VW]); o += GLA_VW
    wg = bf(w_in[:, o:o + GLA_VW]); o += GLA_VW
    w_ha = w_in[:, o:o + GLA_LOWRANK]; o += GLA_LOWRANK
    w_mla = w_in[:, o:]
    pad = SMALL_W - HA_OFF - GLA_LOWRANK
    ws = bf(jnp.concatenate([w_mla, w_ha, jnp.zeros((D, pad), w_in.dtype)], axis=1))
    wa2T = jnp.zeros((GLA_KW, LANES), F32).at[:, MLA_ROPE:MLA_ROPE + GLA_LOWRANK].set(w_a2.T)
    wa2T = bf(wa2T)
    ba2 = b_a2.reshape(-1, 1)

    wuqT = bf(w_uq).T
    w_ukv3 = w_ukv.reshape(MLA_KV_LORA, MLA_HEADS, MLA_NOPE + MLA_V)
    wuk = bf(w_ukv3[:, :, :MLA_NOPE].reshape(MLA_KV_LORA, MLA_HEADS * MLA_NOPE))
    wuvT = bf(w_ukv3[:, :, MLA_NOPE:].reshape(MLA_KV_LORA, MLA_W)).T
    invf = (1.0 / (ROPE_THETA ** (jnp.arange(0, MLA_ROPE, 2, dtype=F32) / MLA_ROPE))).reshape(-1, 1)

    tm = _tile(T, 512)
    q, kT, v, g, laT, small = _in_proj(x2, r2(pre_mix_norm), wq, wkT, wv, wg, ws, wa2T, ba2, tm)
    gla_o = _gla(q, kT, laT, v, g, r2(gla_norm), B, S, _tile(S, 256))

    tq = _tile(S, 256)
    qT, k5, vT5 = _mla_prep(small, pos3, invf, r2(q_norm), wuqT, r2(kv_norm), wuk, wuvT, B, S, tq)
    mla_o = _mla_attn(qT, k5, vT5, B, S, tq).reshape(T, MLA_W)

    x1 = _mix(x2, gla_o, mla_o, r2(pre_mix_norm), bf(w_gate), r2(b_gate), bf(w_o_gla), bf(w_o_mla), bf(w_out),
              r2(post_mix_norm), tm)
    return _ffn(x1, r2(pre_ffn_norm), bf(w_ffn_gate), bf(w_ffn_up), bf(w_ffn_down), r2(post_ffn_norm), tm)


def kernel(x, positions, pre_mix_norm, w_in, w_a2, b_a2, gla_norm, w_o_gla, q_norm, w_uq, kv_norm, w_ukv, w_o_mla,
           w_gate, b_gate, w_out, post_mix_norm, pre_ffn_norm, w_ffn_gate, w_ffn_up, w_ffn_down, post_ffn_norm):
    B, S, D = x.shape
    x2 = x.reshape(B * S, D)
    pos3 = positions.reshape(B, 1, S)
    per_layer = (pre_mix_norm, w_in, w_a2, b_a2, gla_norm, w_o_gla, q_norm, w_uq, kv_norm, w_ukv, w_o_mla, w_gate,
                 b_gate, w_out, post_mix_norm, pre_ffn_norm, w_ffn_gate, w_ffn_up, w_ffn_down, post_ffn_norm)
    for l in range(pre_mix_norm.shape[0]):
        x2 = _layer(x2, pos3, B, S, *[p[l] for p in per_layer])
    return x2.reshape(B, S, D)
```

```python
import functools

import jax
import jax.numpy as jnp
from jax import lax
from jax.experimental import pallas as pl
from jax.experimental.pallas import tpu as pltpu

CHUNK = 64
GLA_HEADS = 4
GLA_DK = 128
GLA_DV = 256
GLA_LOWRANK = 16
GLA_GATE_TEMP = 16.0
MLA_HEADS = 8
MLA_Q_LORA = 384
MLA_KV_LORA = 256
MLA_NOPE = 128
MLA_ROPE = 64
MLA_V = 128
MLA_QK = MLA_NOPE + MLA_ROPE
ROPE_THETA = 10000.0
NORM_EPS = 1e-6

GLA_KW = GLA_HEADS * GLA_DK
GLA_VW = GLA_HEADS * GLA_DV
MLA_W = MLA_HEADS * MLA_V
SMALL_W = 768
KPE_OFF = MLA_Q_LORA + MLA_KV_LORA
HA_OFF = KPE_OFF + MLA_ROPE
LANES = 128

VMEM_LIMIT_BYTES = 56 * 1024 * 1024

BF16 = jnp.bfloat16
F32 = jnp.float32

_NT = (((1,), (1,)), ((), ()))


def _dot(a, b):
    return jnp.dot(a, b, preferred_element_type=F32)


def _dot_nt(a, b):
    return lax.dot_general(a, b, _NT, preferred_element_type=F32)


def _rms(x, gain):
    ms = jnp.mean(x * x, axis=-1, keepdims=True)
    return x * lax.rsqrt(ms + NORM_EPS) * gain


def _sigmoid(z):
    return 1.0 / (1.0 + jnp.exp(-z))


def _const_spec(shape):
    nd = len(shape)
    return pl.BlockSpec(shape, lambda *_: (0,) * nd, pipeline_mode=pl.Buffered(1))


def _params(sem):
    return pltpu.CompilerParams(dimension_semantics=sem, vmem_limit_bytes=VMEM_LIMIT_BYTES)


def _in_proj_kernel(x_ref, gn_ref, wq_ref, wkT_ref, wv_ref, wg_ref, ws_ref, wa2T_ref, ba2_ref,
                    q_ref, kT_ref, v_ref, g_ref, laT_ref, small_ref):
    h = _rms(x_ref[...], gn_ref[...]).astype(BF16)
    q_ref[...] = (_dot(h, wq_ref[...]) * (GLA_DK ** -0.5)).astype(BF16)
    kT_ref[...] = _dot_nt(wkT_ref[...], h).astype(BF16)
    v_ref[...] = _dot(h, wv_ref[...]).astype(BF16)
    g_ref[...] = _dot(h, wg_ref[...]).astype(BF16)
    small = _dot(h, ws_ref[...])
    small_ref[...] = small
    last = small[:, KPE_OFF:].astype(BF16)
    zT = _dot_nt(wa2T_ref[...], last) + ba2_ref[...]
    log_sig = jnp.minimum(zT, 0.0) - jnp.log(1.0 + jnp.exp(-jnp.abs(zT)))
    laT_ref[...] = log_sig * (1.0 / GLA_GATE_TEMP)


def _in_proj(x2, gn, wq, wkT, wv, wg, ws, wa2T, ba2, tm):
    T, D = x2.shape
    grid = (T // tm,)
    row = lambda i: (i, 0)
    col = lambda i: (0, i)
    return pl.pallas_call(
        _in_proj_kernel,
        grid=grid,
        in_specs=[
            pl.BlockSpec((tm, D), row),
            _const_spec(gn.shape), _const_spec(wq.shape), _const_spec(wkT.shape),
            _const_spec(wv.shape), _const_spec(wg.shape), _const_spec(ws.shape),
            _const_spec(wa2T.shape), _const_spec(ba2.shape),
        ],
        out_specs=[
            pl.BlockSpec((tm, GLA_KW), row),
            pl.BlockSpec((GLA_KW, tm), col),
            pl.BlockSpec((tm, GLA_VW), row),
            pl.BlockSpec((tm, GLA_VW), row),
            pl.BlockSpec((GLA_KW, tm), col),
            pl.BlockSpec((tm, SMALL_W), row),
        ],
        out_shape=[
            jax.ShapeDtypeStruct((T, GLA_KW), BF16),
            jax.ShapeDtypeStruct((GLA_KW, T), BF16),
            jax.ShapeDtypeStruct((T, GLA_VW), BF16),
            jax.ShapeDtypeStruct((T, GLA_VW), BF16),
            jax.ShapeDtypeStruct((GLA_KW, T), F32),
            jax.ShapeDtypeStruct((T, SMALL_W), F32),
        ],
        compiler_params=_params(("parallel",)),
        name="in_proj",
    )(x2, gn, wq, wkT, wv, wg, ws, wa2T, ba2)


PAIR = 2 * CHUNK


def _gla_kernel(q_ref, kT_ref, laT_ref, v_ref, g_ref, gn_ref, o_ref, state_ref, *, ts):
    @pl.when(pl.program_id(1) == 0)
    def _():
        state_ref[...] = jnp.zeros_like(state_ref)

    row = lax.broadcasted_iota(jnp.int32, (PAIR, PAIR), 0)
    col = lax.broadcasted_iota(jnp.int32, (PAIR, PAIR), 1)
    later = ((row // CHUNK == col // CHUNK) & (row > col)).astype(BF16)
    lane = lax.broadcasted_iota(jnp.int32, (GLA_DK, PAIR), 1)
    first = lane < CHUNK
    gn = gn_ref[...]

    for h in range(GLA_HEADS):
        ks = slice(h * GLA_DK, (h + 1) * GLA_DK)
        vs = slice(h * GLA_DV, (h + 1) * GLA_DV)
        st = state_ref[h]
        for p in range(ts // PAIR):
            tok = slice(p * PAIR, (p + 1) * PAIR)
            la = laT_ref[ks, tok]
            la_hi = la.astype(BF16)
            la_lo = (la - la_hi.astype(F32)).astype(BF16)
            to_end = _dot(la_hi, later) + _dot(la_lo, later)
            kd = kT_ref[ks, tok].astype(F32) * jnp.exp(to_end)
            kd0 = jnp.where(first, kd, 0.0).astype(BF16)
            kd1 = jnp.where(first, 0.0, kd).astype(BF16)
            dec0 = jnp.exp(jnp.sum(jnp.where(first, la, 0.0), axis=1, keepdims=True))
            dec1 = jnp.exp(jnp.sum(jnp.where(first, 0.0, la), axis=1, keepdims=True))
            v = v_ref[tok, vs]
            q = q_ref[tok, ks]
            st = dec0 * st + _dot(kd0, v)
            o0 = _dot(q[:CHUNK], st.astype(BF16))
            st = dec1 * st + _dot(kd1, v)
            o1 = _dot(q[CHUNK:], st.astype(BF16))
            o = jnp.concatenate([o0, o1], axis=0)
            o = _rms(o, gn)
            g = g_ref[tok, vs].astype(F32)
            o_ref[tok, vs] = (o * (g * _sigmoid(g))).astype(BF16)
        state_ref[h] = st


def _gla(q, kT, laT, v, g, gn, B, S, ts):
    T = B * S
    nt = S // ts
    row = lambda b, t: (b * nt + t, 0)
    col = lambda b, t: (0, b * nt + t)
    return pl.pallas_call(
        functools.partial(_gla_kernel, ts=ts),
        grid=(B, nt),
        in_specs=[
            pl.BlockSpec((ts, GLA_KW), row),
            pl.BlockSpec((GLA_KW, ts), col),
            pl.BlockSpec((GLA_KW, ts), col),
            pl.BlockSpec((ts, GLA_VW), row),
            pl.BlockSpec((ts, GLA_VW), row),
            _const_spec(gn.shape),
        ],
        out_specs=pl.BlockSpec((ts, GLA_VW), row),
        out_shape=jax.ShapeDtypeStruct((T, GLA_VW), BF16),
        scratch_shapes=[pltpu.VMEM((GLA_HEADS, GLA_DK, GLA_DV), F32)],
        compiler_params=_params(("parallel", "arbitrary")),
        name="gla",
    )(q, kT, laT, v, g, gn)


def _mla_prep_kernel(small_ref, pos_ref, invf_ref, qn_ref, wuqT_ref, kvn_ref, wuk_ref, wuvT_ref,
                     qT_ref, k_ref, vT_ref, *, tm):
    sm = small_ref[...]
    cq = _rms(sm[:, :MLA_Q_LORA], qn_ref[...]).astype(BF16)
    ckv = _rms(sm[:, MLA_Q_LORA:KPE_OFF], kvn_ref[...]).astype(BF16)
    last = sm[:, KPE_OFF:]

    half = MLA_ROPE // 2
    ang = invf_ref[...] * pos_ref[...].astype(F32)
    cos = jnp.cos(ang)
    sin = jnp.sin(ang)

    scale = MLA_QK ** -0.5
    qT = _dot_nt(wuqT_ref[...], cq)
    for h in range(MLA_HEADS):
        b0 = h * MLA_QK
        qT_ref[b0:b0 + MLA_NOPE, :] = (qT[b0:b0 + MLA_NOPE] * scale).astype(BF16)
        x1 = qT[b0 + MLA_NOPE:b0 + MLA_NOPE + half]
        x2 = qT[b0 + MLA_NOPE + half:b0 + MLA_QK]
        qT_ref[b0 + MLA_NOPE:b0 + MLA_NOPE + half, :] = ((x1 * cos - x2 * sin) * scale).astype(BF16)
        qT_ref[b0 + MLA_NOPE + half:b0 + MLA_QK, :] = ((x2 * cos + x1 * sin) * scale).astype(BF16)

    zeros = jnp.zeros((LANES - MLA_ROPE, tm), F32)
    c_tab = jnp.concatenate([cos, cos, zeros], axis=0).T
    s_tab = jnp.concatenate([-sin, sin, zeros], axis=0).T
    lane = lax.broadcasted_iota(jnp.int32, last.shape, 1)
    swapped = jnp.where(lane < half, pltpu.roll(last, LANES - half, axis=1), pltpu.roll(last, half, axis=1))
    kpe = (last * c_tab + swapped * s_tab)[:, :MLA_ROPE].astype(BF16)

    k_nope = _dot(ckv, wuk_ref[...]).astype(BF16)
    for h in range(MLA_HEADS):
        k_ref[h, :, :MLA_NOPE] = k_nope[:, h * MLA_NOPE:(h + 1) * MLA_NOPE]
        k_ref[h, :, MLA_NOPE:] = kpe
    vT = _dot_nt(wuvT_ref[...], ckv).astype(BF16)
    for h in range(MLA_HEADS):
        vT_ref[h] = vT[h * MLA_V:(h + 1) * MLA_V]


def _mla_prep(small, pos3, invf, qn, wuqT, kvn, wuk, wuvT, B, S, tm):
    nt = S // tm
    return pl.pallas_call(
        functools.partial(_mla_prep_kernel, tm=tm),
        grid=(B, nt),
        in_specs=[
            pl.BlockSpec((tm, SMALL_W), lambda b, t: (b * nt + t, 0)),
            pl.BlockSpec((None, 1, tm), lambda b, t: (b, 0, t)),
            _const_spec(invf.shape), _const_spec(qn.shape), _const_spec(wuqT.shape),
            _const_spec(kvn.shape), _const_spec(wuk.shape), _const_spec(wuvT.shape),
        ],
        out_specs=[
            pl.BlockSpec((None, MLA_HEADS * MLA_QK, tm), lambda b, t: (b, 0, t)),
            pl.BlockSpec((None, MLA_HEADS, None, tm, MLA_QK), lambda b, t: (b, 0, t, 0, 0)),
            pl.BlockSpec((None, MLA_HEADS, None, MLA_V, tm), lambda b, t: (b, 0, t, 0, 0)),
        ],
        out_shape=[
            jax.ShapeDtypeStruct((B, MLA_HEADS * MLA_QK, S), BF16),
            jax.ShapeDtypeStruct((B, MLA_HEADS, nt, tm, MLA_QK), BF16),
            jax.ShapeDtypeStruct((B, MLA_HEADS, nt, MLA_V, tm), BF16),
        ],
        compiler_params=_params(("parallel", "parallel")),
        name="mla_prep",
    )(small, pos3, invf, qn, wuqT, kvn, wuk, wuvT)


def _mla_attn_kernel(qT_ref, k_ref, vT_ref, o_ref, acc_ref, m_ref, l_ref, s0_ref, s1_ref, p0_ref, p1_ref,
                     a0_ref, a1_ref, *, tq, tk):
    i = pl.program_id(2)
    r = tq // tk
    n = i * r
    s_refs, p_refs, a_refs = (s0_ref, s1_ref), (p0_ref, p1_ref), (a0_ref, a1_ref)
    acc_ref[...] = jnp.zeros_like(acc_ref)
    m_ref[...] = jnp.full_like(m_ref, -jnp.inf)
    l_ref[...] = jnp.zeros_like(l_ref)
    p1_ref[...] = jnp.zeros_like(p1_ref)
    a1_ref[...] = jnp.ones_like(a1_ref)
    s0_ref[...] = _dot(k_ref[0], qT_ref[...])

    def softmax(sT, cols):
        m_old = m_ref[:, cols]
        m_new = jnp.maximum(m_old, jnp.max(sT, axis=0, keepdims=True))
        alpha = jnp.exp(m_old - m_new)
        p = jnp.exp(sT - m_new)
        l_ref[:, cols] = alpha * l_ref[:, cols] + jnp.sum(p, axis=0, keepdims=True)
        m_ref[:, cols] = m_new
        return alpha, p.astype(BF16)

    def pipelined(j, slot):
        s_cur, p_cur, a_cur = s_refs[slot], p_refs[slot], a_refs[slot]
        s_nxt, p_prv, a_prv = s_refs[1 - slot], p_refs[1 - slot], a_refs[1 - slot]
        k_next = k_ref[j + 1]
        vT_prev = vT_ref[jnp.maximum(j - 1, 0)]
        for c in range(tq // tk):
            cols = slice(c * tk, (c + 1) * tk)
            s_nxt[:, cols] = _dot(k_next, qT_ref[:, cols])
            acc_ref[:, cols] = a_prv[:, cols] * acc_ref[:, cols] + _dot(vT_prev, p_prv[:, cols])
            alpha, p = softmax(s_cur[:, cols], cols)
            a_cur[:, cols] = alpha
            p_cur[:, cols] = p

    def body(jj, carry):
        pipelined(2 * jj, 0)
        pipelined(2 * jj + 1, 1)
        return carry

    lax.fori_loop(0, n // 2, body, 0)
    acc_ref[...] = a1_ref[...] * acc_ref[...] + _dot(vT_ref[jnp.maximum(n - 1, 0)], p1_ref[...])
    for d in range(r):
        c0 = d * tk
        sT = s0_ref[...] if d == 0 else _dot(k_ref[n + d], qT_ref[:, c0:])
        kc = lax.broadcasted_iota(jnp.int32, sT.shape, 0) // CHUNK
        qc = lax.broadcasted_iota(jnp.int32, sT.shape, 1) // CHUNK
        alpha, p = softmax(jnp.where(kc <= qc, sT, -jnp.inf), slice(c0, tq))
        acc_ref[:, c0:] = alpha * acc_ref[:, c0:] + _dot(vT_ref[n + d], p)
    o_ref[...] = (acc_ref[...] / l_ref[...]).T.astype(o_ref.dtype)


def _mla_attn(qT, k5, vT5, B, S, tq):
    nk, tk = k5.shape[2], k5.shape[3]
    return pl.pallas_call(
        functools.partial(_mla_attn_kernel, tq=tq, tk=tk),
        grid=(B, MLA_HEADS, S // tq),
        in_specs=[
            pl.BlockSpec((None, MLA_QK, tq), lambda b, h, i: (b, h, i)),
            pl.BlockSpec((None, None, nk, tk, MLA_QK), lambda b, h, i: (b, h, 0, 0, 0)),
            pl.BlockSpec((None, None, nk, MLA_V, tk), lambda b, h, i: (b, h, 0, 0, 0)),
        ],
        out_specs=pl.BlockSpec((None, tq, MLA_V), lambda b, h, i: (b, i, h)),
        out_shape=jax.ShapeDtypeStruct((B, S, MLA_W), BF16),
        scratch_shapes=[
            pltpu.VMEM((MLA_V, tq), F32), pltpu.VMEM((1, tq), F32), pltpu.VMEM((1, tq), F32),
            pltpu.VMEM((tk, tq), F32), pltpu.VMEM((tk, tq), F32),
            pltpu.VMEM((tk, tq), BF16), pltpu.VMEM((tk, tq), BF16),
            pltpu.VMEM((1, tq), F32), pltpu.VMEM((1, tq), F32),
        ],
        compiler_params=_params(("parallel", "parallel", "arbitrary")),
        name="mla_attn",
    )(qT, k5, vT5)


def _mix_kernel(x_ref, a_ref, b_ref, gn_ref, wgate_ref, bgate_ref, woa_ref, wob_ref, wout_ref, pn_ref, o_ref):
    x = x_ref[...]
    d = x.shape[-1]
    h = _rms(x, gn_ref[...]).astype(BF16)
    gates = _sigmoid(_dot(h, wgate_ref[...]) + bgate_ref[...])
    y_a = _dot(a_ref[...], woa_ref[...])
    y_b = _dot(b_ref[...], wob_ref[...])
    mixed = (gates[:, :d] * y_a + gates[:, d:] * y_b).astype(BF16)
    o_ref[...] = x + _rms(_dot(mixed, wout_ref[...]), pn_ref[...])


def _mix(x2, a, b, gn, wgate, bgate, woa, wob, wout, pn, tm):
    T, D = x2.shape
    row = lambda i: (i, 0)
    return pl.pallas_call(
        _mix_kernel,
        grid=(T // tm,),
        in_specs=[
            pl.BlockSpec((tm, D), row), pl.BlockSpec((tm, a.shape[1]), row), pl.BlockSpec((tm, b.shape[1]), row),
            _const_spec(gn.shape), _const_spec(wgate.shape), _const_spec(bgate.shape),
            _const_spec(woa.shape), _const_spec(wob.shape), _const_spec(wout.shape), _const_spec(pn.shape),
        ],
        out_specs=pl.BlockSpec((tm, D), row),
        out_shape=jax.ShapeDtypeStruct((T, D), F32),
        compiler_params=_params(("parallel",)),
        name="mix",
    )(x2, a, b, gn, wgate, bgate, woa, wob, wout, pn)


def _ffn_kernel(x_ref, gn_ref, wg_ref, wu_ref, wd_ref, pn_ref, o_ref):
    x = x_ref[...]
    h = _rms(x, gn_ref[...]).astype(BF16)
    gate = _dot(h, wg_ref[...])
    up = _dot(h, wu_ref[...])
    act = (gate * _sigmoid(gate) * up).astype(BF16)
    o_ref[...] = x + _rms(_dot(act, wd_ref[...]), pn_ref[...])


def _ffn(x2, gn, wg, wu, wd, pn, tm):
    T, D = x2.shape
    row = lambda i: (i, 0)
    return pl.pallas_call(
        _ffn_kernel,
        grid=(T // tm,),
        in_specs=[
            pl.BlockSpec((tm, D), row),
            _const_spec(gn.shape), _const_spec(wg.shape), _const_spec(wu.shape),
            _const_spec(wd.shape), _const_spec(pn.shape),
        ],
        out_specs=pl.BlockSpec((tm, D), row),
        out_shape=jax.ShapeDtypeStruct((T, D), F32),
        compiler_params=_params(("parallel",)),
        name="ffn",
    )(x2, gn, wg, wu, wd, pn)


def _tile(n, pref):
    t = min(n, pref)
    assert n % t == 0, (n, t)
    return t


def _layer(x2, pos3, B, S, pre_mix_norm, w_in, w_a2, b_a2, gla_norm, w_o_gla, q_norm, w_uq, kv_norm, w_ukv,
           w_o_mla, w_gate, b_gate, w_out, post_mix_norm, pre_ffn_norm, w_ffn_gate, w_ffn_up, w_ffn_down,
           post_ffn_norm):
    T, D = x2.shape
    bf = lambda w: w.astype(BF16)
    r2 = lambda v: v.reshape(1, -1)

    o = 0
    wq = bf(w_in[:, o:o + GLA_KW]); o += GLA_KW
    wkT = bf(w_in[:, o:o + GLA_KW]).T; o += GLA_KW
    wv = bf(w_in[:, o:o + GLA_VW]); o += GLA_VW
    wg = bf(w_in[:, o:o + GLA_VW]); o += GLA_VW
    w_ha = w_in[:, o:o + GLA_LOWRANK]; o += GLA_LOWRANK
    w_mla = w_in[:, o:]
    pad = SMALL_W - HA_OFF - GLA_LOWRANK
    ws = bf(jnp.concatenate([w_mla, w_ha, jnp.zeros((D, pad), w_in.dtype)], axis=1))
    wa2T = jnp.zeros((GLA_KW, LANES), F32).at[:, MLA_ROPE:MLA_ROPE + GLA_LOWRANK].set(w_a2.T)
    wa2T = bf(wa2T)
    ba2 = b_a2.reshape(-1, 1)

    wuqT = bf(w_uq).T
    w_ukv3 = w_ukv.reshape(MLA_KV_LORA, MLA_HEADS, MLA_NOPE + MLA_V)
    wuk = bf(w_ukv3[:, :, :MLA_NOPE].reshape(MLA_KV_LORA, MLA_HEADS * MLA_NOPE))
    wuvT = bf(w_ukv3[:, :, MLA_NOPE:].reshape(MLA_KV_LORA, MLA_W)).T
    invf = (1.0 / (ROPE_THETA ** (jnp.arange(0, MLA_ROPE, 2, dtype=F32) / MLA_ROPE))).reshape(-1, 1)

    tm = _tile(T, 512)
    q, kT, v, g, laT, small = _in_proj(x2, r2(pre_mix_norm), wq, wkT, wv, wg, ws, wa2T, ba2, tm)
    gla_o = _gla(q, kT, laT, v, g, r2(gla_norm), B, S, _tile(S, 256))

    tk = _tile(S, 256)
    qT, k5, vT5 = _mla_prep(small, pos3, invf, r2(q_norm), wuqT, r2(kv_norm), wuk, wuvT, B, S, tk)
    mla_o = _mla_attn(qT, k5, vT5, B, S, _tile(S, 1024)).reshape(T, MLA_W)

    x1 = _mix(x2, gla_o, mla_o, r2(pre_mix_norm), bf(w_gate), r2(b_gate), bf(w_o_gla), bf(w_o_mla), bf(w_out),
              r2(post_mix_norm), tm)
    return _ffn(x1, r2(pre_ffn_norm), bf(w_ffn_gate), bf(w_ffn_up), bf(w_ffn_down), r2(post_ffn_norm), tm)


def kernel(x, positions, pre_mix_norm, w_in, w_a2, b_a2, gla_norm, w_o_gla, q_norm, w_uq, kv_norm, w_ukv, w_o_mla,
           w_gate, b_gate, w_out, post_mix_norm, pre_ffn_norm, w_ffn_gate, w_ffn_up, w_ffn_down, post_ffn_norm):
    B, S, D = x.shape
    x2 = x.reshape(B * S, D)
    pos3 = positions.reshape(B, 1, S)
    per_layer = (pre_mix_norm, w_in, w_a2, b_a2, gla_norm, w_o_gla, q_norm, w_uq, kv_norm, w_ukv, w_o_mla, w_gate,
                 b_gate, w_out, post_mix_norm, pre_ffn_norm, w_ffn_gate, w_ffn_up, w_ffn_down, post_ffn_norm)
    for l in range(pre_mix_norm.shape[0]):
        x2 = _layer(x2, pos3, B, S, *[p[l] for p in per_layer])
    return x2.reshape(B, S, D)
```

```python
import functools

import jax
import jax.numpy as jnp
from jax import lax
from jax.experimental import pallas as pl
from jax.experimental.pallas import tpu as pltpu

CHUNK = 64
GLA_HEADS = 4
GLA_DK = 128
GLA_DV = 256
GLA_LOWRANK = 16
GLA_GATE_TEMP = 16.0
MLA_HEADS = 8
MLA_Q_LORA = 384
MLA_KV_LORA = 256
MLA_NOPE = 128
MLA_ROPE = 64
MLA_V = 128
MLA_QK = MLA_NOPE + MLA_ROPE
ROPE_THETA = 10000.0
NORM_EPS = 1e-6

GLA_KW = GLA_HEADS * GLA_DK
GLA_VW = GLA_HEADS * GLA_DV
MLA_W = MLA_HEADS * MLA_V
SMALL_W = 768
KPE_OFF = MLA_Q_LORA + MLA_KV_LORA
HA_OFF = KPE_OFF + MLA_ROPE
LANES = 128
K_AUG = 256
V_AUG = MLA_V + 16
MASK_BIAS = float(jnp.finfo(jnp.bfloat16).min)
LOG2E = 1.4426950408889634

VMEM_LIMIT_BYTES = 56 * 1024 * 1024

BF16 = jnp.bfloat16
F32 = jnp.float32

_NT = (((1,), (1,)), ((), ()))


def _dot(a, b):
    return jnp.dot(a, b, preferred_element_type=F32)


def _dot_nt(a, b):
    return lax.dot_general(a, b, _NT, preferred_element_type=F32)


def _rms(x, gain):
    ms = jnp.mean(x * x, axis=-1, keepdims=True)
    return x * lax.rsqrt(ms + NORM_EPS) * gain


def _sigmoid(z):
    return 1.0 / (1.0 + jnp.exp(-z))


def _const_spec(shape):
    nd = len(shape)
    return pl.BlockSpec(shape, lambda *_: (0,) * nd, pipeline_mode=pl.Buffered(1))


def _params(sem):
    return pltpu.CompilerParams(dimension_semantics=sem, vmem_limit_bytes=VMEM_LIMIT_BYTES)


def _in_proj_kernel(x_ref, gn_ref, wq_ref, wkT_ref, wv_ref, wg_ref, ws_ref, wa2T_ref, ba2_ref,
                    q_ref, kT_ref, v_ref, g_ref, laT_ref, small_ref):
    h = _rms(x_ref[...], gn_ref[...]).astype(BF16)
    q_ref[...] = (_dot(h, wq_ref[...]) * (GLA_DK ** -0.5)).astype(BF16)
    kT_ref[...] = _dot_nt(wkT_ref[...], h).astype(BF16)
    v_ref[...] = _dot(h, wv_ref[...]).astype(BF16)
    g_ref[...] = _dot(h, wg_ref[...]).astype(BF16)
    small = _dot(h, ws_ref[...])
    small_ref[...] = small
    last = small[:, KPE_OFF:].astype(BF16)
    zT = _dot_nt(wa2T_ref[...], last) + ba2_ref[...]
    log_sig = jnp.minimum(zT, 0.0) - jnp.log(1.0 + jnp.exp(-jnp.abs(zT)))
    laT_ref[...] = log_sig * (1.0 / GLA_GATE_TEMP)


def _in_proj(x2, gn, wq, wkT, wv, wg, ws, wa2T, ba2, tm):
    T, D = x2.shape
    grid = (T // tm,)
    row = lambda i: (i, 0)
    col = lambda i: (0, i)
    return pl.pallas_call(
        _in_proj_kernel,
        grid=grid,
        in_specs=[
            pl.BlockSpec((tm, D), row),
            _const_spec(gn.shape), _const_spec(wq.shape), _const_spec(wkT.shape),
            _const_spec(wv.shape), _const_spec(wg.shape), _const_spec(ws.shape),
            _const_spec(wa2T.shape), _const_spec(ba2.shape),
        ],
        out_specs=[
            pl.BlockSpec((tm, GLA_KW), row),
            pl.BlockSpec((GLA_KW, tm), col),
            pl.BlockSpec((tm, GLA_VW), row),
            pl.BlockSpec((tm, GLA_VW), row),
            pl.BlockSpec((GLA_KW, tm), col),
            pl.BlockSpec((tm, SMALL_W), row),
        ],
        out_shape=[
            jax.ShapeDtypeStruct((T, GLA_KW), BF16),
            jax.ShapeDtypeStruct((GLA_KW, T), BF16),
            jax.ShapeDtypeStruct((T, GLA_VW), BF16),
            jax.ShapeDtypeStruct((T, GLA_VW), BF16),
            jax.ShapeDtypeStruct((GLA_KW, T), F32),
            jax.ShapeDtypeStruct((T, SMALL_W), F32),
        ],
        compiler_params=_params(("parallel",)),
        name="in_proj",
    )(x2, gn, wq, wkT, wv, wg, ws, wa2T, ba2)


PAIR = 2 * CHUNK


def _gla_kernel(q_ref, kT_ref, laT_ref, v_ref, g_ref, gn_ref, o_ref, state_ref, *, ts):
    @pl.when(pl.program_id(1) == 0)
    def _():
        state_ref[...] = jnp.zeros_like(state_ref)

    row = lax.broadcasted_iota(jnp.int32, (PAIR, PAIR), 0)
    col = lax.broadcasted_iota(jnp.int32, (PAIR, PAIR), 1)
    later = ((row // CHUNK == col // CHUNK) & (row > col)).astype(BF16)
    lane = lax.broadcasted_iota(jnp.int32, (GLA_DK, PAIR), 1)
    first = lane < CHUNK
    gn = gn_ref[...]

    for h in range(GLA_HEADS):
        ks = slice(h * GLA_DK, (h + 1) * GLA_DK)
        vs = slice(h * GLA_DV, (h + 1) * GLA_DV)
        st = state_ref[h]
        for p in range(ts // PAIR):
            tok = slice(p * PAIR, (p + 1) * PAIR)
            la = laT_ref[ks, tok]
            la_hi = la.astype(BF16)
            la_lo = (la - la_hi.astype(F32)).astype(BF16)
            to_end = _dot(la_hi, later) + _dot(la_lo, later)
            kd = kT_ref[ks, tok].astype(F32) * jnp.exp(to_end)
            kd0 = jnp.where(first, kd, 0.0).astype(BF16)
            kd1 = jnp.where(first, 0.0, kd).astype(BF16)
            dec0 = jnp.exp(jnp.sum(jnp.where(first, la, 0.0), axis=1, keepdims=True))
            dec1 = jnp.exp(jnp.sum(jnp.where(first, 0.0, la), axis=1, keepdims=True))
            v = v_ref[tok, vs]
            q = q_ref[tok, ks]
            st = dec0 * st + _dot(kd0, v)
            o0 = _dot(q[:CHUNK], st.astype(BF16))
            st = dec1 * st + _dot(kd1, v)
            o1 = _dot(q[CHUNK:], st.astype(BF16))
            o = jnp.concatenate([o0, o1], axis=0)
            o = _rms(o, gn)
            g = g_ref[tok, vs].astype(F32)
            o_ref[tok, vs] = (o * (g * _sigmoid(g))).astype(BF16)
        state_ref[h] = st


def _gla(q, kT, laT, v, g, gn, B, S, ts):
    T = B * S
    nt = S // ts
    row = lambda b, t: (b * nt + t, 0)
    col = lambda b, t: (0, b * nt + t)
    return pl.pallas_call(
        functools.partial(_gla_kernel, ts=ts),
        grid=(B, nt),
        in_specs=[
            pl.BlockSpec((ts, GLA_KW), row),
            pl.BlockSpec((GLA_KW, ts), col),
            pl.BlockSpec((GLA_KW, ts), col),
            pl.BlockSpec((ts, GLA_VW), row),
            pl.BlockSpec((ts, GLA_VW), row),
            _const_spec(gn.shape),
        ],
        out_specs=pl.BlockSpec((ts, GLA_VW), row),
        out_shape=jax.ShapeDtypeStruct((T, GLA_VW), BF16),
        scratch_shapes=[pltpu.VMEM((GLA_HEADS, GLA_DK, GLA_DV), F32)],
        compiler_params=_params(("parallel", "arbitrary")),
        name="gla",
    )(q, kT, laT, v, g, gn)


def _mla_prep_kernel(small_ref, pos_ref, invf_ref, qn_ref, wuqT_ref, kvn_ref, wuk_ref, wuvT_ref,
                     qT_ref, k_ref, vT_ref, *, tm, tiles_per_q):
    sm = small_ref[...]
    cq = _rms(sm[:, :MLA_Q_LORA], qn_ref[...]).astype(BF16)
    ckv = _rms(sm[:, MLA_Q_LORA:KPE_OFF], kvn_ref[...]).astype(BF16)
    last = sm[:, KPE_OFF:]

    half = MLA_ROPE // 2
    ang = invf_ref[...] * pos_ref[...].astype(F32)
    cos = jnp.cos(ang)
    sin = jnp.sin(ang)

    scale = MLA_QK ** -0.5 * LOG2E
    qT = _dot_nt(wuqT_ref[...], cq)
    for h in range(MLA_HEADS):
        b0 = h * MLA_QK
        qT_ref[b0:b0 + MLA_NOPE, :] = (qT[b0:b0 + MLA_NOPE] * scale).astype(BF16)
        x1 = qT[b0 + MLA_NOPE:b0 + MLA_NOPE + half]
        x2 = qT[b0 + MLA_NOPE + half:b0 + MLA_QK]
        qT_ref[b0 + MLA_NOPE:b0 + MLA_NOPE + half, :] = ((x1 * cos - x2 * sin) * scale).astype(BF16)
        qT_ref[b0 + MLA_NOPE + half:b0 + MLA_QK, :] = ((x2 * cos + x1 * sin) * scale).astype(BF16)

    zeros = jnp.zeros((LANES - MLA_ROPE, tm), F32)
    c_tab = jnp.concatenate([cos, cos, zeros], axis=0).T
    s_tab = jnp.concatenate([-sin, sin, zeros], axis=0).T
    lane = lax.broadcasted_iota(jnp.int32, last.shape, 1)
    swapped = jnp.where(lane < half, pltpu.roll(last, LANES - half, axis=1), pltpu.roll(last, half, axis=1))
    kpe = (last * c_tab + swapped * s_tab)[:, :MLA_ROPE].astype(BF16)

    first_chunk = (pl.program_id(1) % tiles_per_q) * (tm // CHUNK)
    e = lax.broadcasted_iota(jnp.int32, (tm, K_AUG - MLA_QK), 1)
    chunk = lax.broadcasted_iota(jnp.int32, (tm, K_AUG - MLA_QK), 0) // CHUNK + first_chunk
    onehot = (e == chunk).astype(BF16)

    k_nope = _dot(ckv, wuk_ref[...]).astype(BF16)
    for h in range(MLA_HEADS):
        k_ref[h, :, :MLA_NOPE] = k_nope[:, h * MLA_NOPE:(h + 1) * MLA_NOPE]
        k_ref[h, :, MLA_NOPE:MLA_QK] = kpe
        k_ref[h, :, MLA_QK:] = onehot
    vT = _dot_nt(wuvT_ref[...], ckv).astype(BF16)
    ones = jnp.ones((V_AUG - MLA_V, tm), BF16)
    for h in range(MLA_HEADS):
        vT_ref[h, :MLA_V, :] = vT[h * MLA_V:(h + 1) * MLA_V]
        vT_ref[h, MLA_V:, :] = ones


def _mla_prep(small, pos3, invf, qn, wuqT, kvn, wuk, wuvT, B, S, tm, tiles_per_q):
    nt = S // tm
    return pl.pallas_call(
        functools.partial(_mla_prep_kernel, tm=tm, tiles_per_q=tiles_per_q),
        grid=(B, nt),
        in_specs=[
            pl.BlockSpec((tm, SMALL_W), lambda b, t: (b * nt + t, 0)),
            pl.BlockSpec((None, 1, tm), lambda b, t: (b, 0, t)),
            _const_spec(invf.shape), _const_spec(qn.shape), _const_spec(wuqT.shape),
            _const_spec(kvn.shape), _const_spec(wuk.shape), _const_spec(wuvT.shape),
        ],
        out_specs=[
            pl.BlockSpec((None, MLA_HEADS * MLA_QK, tm), lambda b, t: (b, 0, t)),
            pl.BlockSpec((None, MLA_HEADS, None, tm, K_AUG), lambda b, t: (b, 0, t, 0, 0)),
            pl.BlockSpec((None, MLA_HEADS, None, V_AUG, tm), lambda b, t: (b, 0, t, 0, 0)),
        ],
        out_shape=[
            jax.ShapeDtypeStruct((B, MLA_HEADS * MLA_QK, S), BF16),
            jax.ShapeDtypeStruct((B, MLA_HEADS, nt, tm, K_AUG), BF16),
            jax.ShapeDtypeStruct((B, MLA_HEADS, nt, V_AUG, tm), BF16),
        ],
        compiler_params=_params(("parallel", "parallel")),
        name="mla_prep",
    )(small, pos3, invf, qn, wuqT, kvn, wuk, wuvT)


def _colmax(x):
    while x.shape[0] > 8 and x.shape[0] % 16 == 0:
        half = x.shape[0] // 2
        x = jnp.maximum(x[:half], x[half:])
    return jnp.max(x, axis=0, keepdims=True)


def _mla_attn_kernel(qT_ref, k_ref, vT_ref, o_ref, qa_ref, acc_ref, m_ref, s0_ref, s1_ref, x0_ref, x1_ref,
                     p0_ref, p1_ref, a0_ref, a1_ref, *, tq, tk):
    i = pl.program_id(2)
    r = tq // tk
    n = i * r
    s_refs, x_refs, p_refs, a_refs = (s0_ref, s1_ref), (x0_ref, x1_ref), (p0_ref, p1_ref), (a0_ref, a1_ref)

    qT = qT_ref[...]
    e = lax.broadcasted_iota(jnp.int32, (K_AUG - MLA_QK, tq), 0)
    q_chunk = lax.broadcasted_iota(jnp.int32, (K_AUG - MLA_QK, tq), 1) // CHUNK
    qa_ref[0, :MLA_QK, :] = qT
    qa_ref[0, MLA_QK:, :] = jnp.zeros((K_AUG - MLA_QK, tq), BF16)
    qa_ref[1, :MLA_QK, :] = qT
    qa_ref[1, MLA_QK:, :] = jnp.where(e > q_chunk, MASK_BIAS, 0.0).astype(BF16)

    acc_ref[...] = jnp.zeros_like(acc_ref)
    m_ref[...] = jnp.full_like(m_ref, -jnp.inf)
    p1_ref[...] = jnp.zeros_like(p1_ref)
    a1_ref[...] = jnp.ones_like(a1_ref)
    s_first = _dot(k_ref[0], qa_ref[(n == 0).astype(jnp.int32)])
    s0_ref[...] = s_first
    x0_ref[...] = _colmax(s_first)

    def stage(j, slot, next_sel, c_prev, c_cur, c_next):
        s_cur, x_cur, p_cur, a_cur = s_refs[slot], x_refs[slot], p_refs[slot], a_refs[slot]
        s_nxt, x_nxt, p_prv, a_prv = s_refs[1 - slot], x_refs[1 - slot], p_refs[1 - slot], a_refs[1 - slot]
        for c in range(r):
            cols = slice(c * tk, (c + 1) * tk)
            if c_next is not None and c * tk >= c_next:
                sT = _dot(k_ref[j + 1], qa_ref[next_sel, :, cols])
                s_nxt[:, cols] = sT
                x_nxt[:, cols] = _colmax(sT)
            if c_prev is not None and c * tk >= c_prev:
                pv = _dot(vT_ref[jnp.maximum(j - 1, 0)], p_prv[:, cols])
                acc_ref[:, cols] = a_prv[:, cols] * acc_ref[:, cols] + pv
            if c_cur is not None and c * tk >= c_cur:
                m_old = m_ref[:, cols]
                m_new = jnp.maximum(m_old, x_cur[:, cols])
                a_cur[:, cols] = jnp.exp2(m_old - m_new)
                p_cur[:, cols] = jnp.exp2(s_cur[:, cols] - m_new).astype(BF16)
                m_ref[:, cols] = m_new

    def body(jj, carry):
        for u in range(r):
            j = r * jj + u
            stage(j, u % 2, (j + 1 == n).astype(jnp.int32), 0, 0, 0)
        return carry

    lax.fori_loop(0, i, body, 0)
    for d in range(r):
        stage(n + d, d % 2, 1, max(d - 1, 0) * tk, d * tk, (d + 1) * tk if d + 1 < r else None)
    stage(n + r, r % 2, 1, (r - 1) * tk, None, None)
    o_ref[...] = (acc_ref[:MLA_V, :] / acc_ref[MLA_V:MLA_V + 1, :]).T.astype(o_ref.dtype)


def _mla_attn(qT, k5, vT5, B, S, tq):
    nk, tk = k5.shape[2], k5.shape[3]
    assert (tq // tk) % 2 == 0 and tq // CHUNK <= K_AUG - MLA_QK
    return pl.pallas_call(
        functools.partial(_mla_attn_kernel, tq=tq, tk=tk),
        grid=(B, MLA_HEADS, S // tq),
        in_specs=[
            pl.BlockSpec((None, MLA_QK, tq), lambda b, h, i: (b, h, i)),
            pl.BlockSpec((None, None, nk, tk, K_AUG), lambda b, h, i: (b, h, 0, 0, 0)),
            pl.BlockSpec((None, None, nk, V_AUG, tk), lambda b, h, i: (b, h, 0, 0, 0)),
        ],
        out_specs=pl.BlockSpec((None, tq, MLA_V), lambda b, h, i: (b, i, h)),
        out_shape=jax.ShapeDtypeStruct((B, S, MLA_W), BF16),
        scratch_shapes=[
            pltpu.VMEM((2, K_AUG, tq), BF16), pltpu.VMEM((V_AUG, tq), F32), pltpu.VMEM((1, tq), F32),
            pltpu.VMEM((tk, tq), F32), pltpu.VMEM((tk, tq), F32),
            pltpu.VMEM((1, tq), F32), pltpu.VMEM((1, tq), F32),
            pltpu.VMEM((tk, tq), BF16), pltpu.VMEM((tk, tq), BF16),
            pltpu.VMEM((1, tq), F32), pltpu.VMEM((1, tq), F32),
        ],
        compiler_params=_params(("parallel", "parallel", "arbitrary")),
        name="mla_attn",
    )(qT, k5, vT5)


def _mix_kernel(x_ref, a_ref, b_ref, gn_ref, wgate_ref, bgate_ref, woa_ref, wob_ref, wout_ref, pn_ref, o_ref):
    x = x_ref[...]
    d = x.shape[-1]
    h = _rms(x, gn_ref[...]).astype(BF16)
    gates = _sigmoid(_dot(h, wgate_ref[...]) + bgate_ref[...])
    y_a = _dot(a_ref[...], woa_ref[...])
    y_b = _dot(b_ref[...], wob_ref[...])
    mixed = (gates[:, :d] * y_a + gates[:, d:] * y_b).astype(BF16)
    o_ref[...] = x + _rms(_dot(mixed, wout_ref[...]), pn_ref[...])


def _mix(x2, a, b, gn, wgate, bgate, woa, wob, wout, pn, tm):
    T, D = x2.shape
    row = lambda i: (i, 0)
    return pl.pallas_call(
        _mix_kernel,
        grid=(T // tm,),
        in_specs=[
            pl.BlockSpec((tm, D), row), pl.BlockSpec((tm, a.shape[1]), row), pl.BlockSpec((tm, b.shape[1]), row),
            _const_spec(gn.shape), _const_spec(wgate.shape), _const_spec(bgate.shape),
            _const_spec(woa.shape), _const_spec(wob.shape), _const_spec(wout.shape), _const_spec(pn.shape),
        ],
        out_specs=pl.BlockSpec((tm, D), row),
        out_shape=jax.ShapeDtypeStruct((T, D), F32),
        compiler_params=_params(("parallel",)),
        name="mix",
    )(x2, a, b, gn, wgate, bgate, woa, wob, wout, pn)


def _ffn_kernel(x_ref, gn_ref, wg_ref, wu_ref, wd_ref, pn_ref, o_ref):
    x = x_ref[...]
    h = _rms(x, gn_ref[...]).astype(BF16)
    gate = _dot(h, wg_ref[...])
    up = _dot(h, wu_ref[...])
    act = (gate * _sigmoid(gate) * up).astype(BF16)
    o_ref[...] = x + _rms(_dot(act, wd_ref[...]), pn_ref[...])


def _ffn(x2, gn, wg, wu, wd, pn, tm):
    T, D = x2.shape
    row = lambda i: (i, 0)
    return pl.pallas_call(
        _ffn_kernel,
        grid=(T // tm,),
        in_specs=[
            pl.BlockSpec((tm, D), row),
            _const_spec(gn.shape), _const_spec(wg.shape), _const_spec(wu.shape),
            _const_spec(wd.shape), _const_spec(pn.shape),
        ],
        out_specs=pl.BlockSpec((tm, D), row),
        out_shape=jax.ShapeDtypeStruct((T, D), F32),
        compiler_params=_params(("parallel",)),
        name="ffn",
    )(x2, gn, wg, wu, wd, pn)


def _tile(n, pref):
    t = min(n, pref)
    assert n % t == 0, (n, t)
    return t


def _layer(x2, pos3, B, S, pre_mix_norm, w_in, w_a2, b_a2, gla_norm, w_o_gla, q_norm, w_uq, kv_norm, w_ukv,
           w_o_mla, w_gate, b_gate, w_out, post_mix_norm, pre_ffn_norm, w_ffn_gate, w_ffn_up, w_ffn_down,
           post_ffn_norm):
    T, D = x2.shape
    bf = lambda w: w.astype(BF16)
    r2 = lambda v: v.reshape(1, -1)

    o = 0
    wq = bf(w_in[:, o:o + GLA_KW]); o += GLA_KW
    wkT = bf(w_in[:, o:o + GLA_KW]).T; o += GLA_KW
    wv = bf(w_in[:, o:o + GLA_VW]); o += GLA_VW
    wg = bf(w_in[:, o:o + GLA_VW]); o += GLA_VW
    w_ha = w_in[:, o:o + GLA_LOWRANK]; o += GLA_LOWRANK
    w_mla = w_in[:, o:]
    pad = SMALL_W - HA_OFF - GLA_LOWRANK
    ws = bf(jnp.concatenate([w_mla, w_ha, jnp.zeros((D, pad), w_in.dtype)], axis=1))
    wa2T = jnp.zeros((GLA_KW, LANES), F32).at[:, MLA_ROPE:MLA_ROPE + GLA_LOWRANK].set(w_a2.T)
    wa2T = bf(wa2T)
    ba2 = b_a2.reshape(-1, 1)

    wuqT = bf(w_uq).T
    w_ukv3 = w_ukv.reshape(MLA_KV_LORA, MLA_HEADS, MLA_NOPE + MLA_V)
    wuk = bf(w_ukv3[:, :, :MLA_NOPE].reshape(MLA_KV_LORA, MLA_HEADS * MLA_NOPE))
    wuvT = bf(w_ukv3[:, :, MLA_NOPE:].reshape(MLA_KV_LORA, MLA_W)).T
    invf = (1.0 / (ROPE_THETA ** (jnp.arange(0, MLA_ROPE, 2, dtype=F32) / MLA_ROPE))).reshape(-1, 1)

    tm = _tile(T, 512)
    q, kT, v, g, laT, small = _in_proj(x2, r2(pre_mix_norm), wq, wkT, wv, wg, ws, wa2T, ba2, tm)
    gla_o = _gla(q, kT, laT, v, g, r2(gla_norm), B, S, _tile(S, 256))

    tk = _tile(S, 256)
    tq = _tile(S, 2048)
    qT, k5, vT5 = _mla_prep(small, pos3, invf, r2(q_norm), wuqT, r2(kv_norm), wuk, wuvT, B, S, tk, tq // tk)
    mla_o = _mla_attn(qT, k5, vT5, B, S, tq).reshape(T, MLA_W)

    x1 = _mix(x2, gla_o, mla_o, r2(pre_mix_norm), bf(w_gate), r2(b_gate), bf(w_o_gla), bf(w_o_mla), bf(w_out),
              r2(post_mix_norm), tm)
    return _ffn(x1, r2(pre_ffn_norm), bf(w_ffn_gate), bf(w_ffn_up), bf(w_ffn_down), r2(post_ffn_norm), tm)


def kernel(x, positions, pre_mix_norm, w_in, w_a2, b_a2, gla_norm, w_o_gla, q_norm, w_uq, kv_norm, w_ukv, w_o_mla,
           w_gate, b_gate, w_out, post_mix_norm, pre_ffn_norm, w_ffn_gate, w_ffn_up, w_ffn_down, post_ffn_norm):
    B, S, D = x.shape
    x2 = x.reshape(B * S, D)
    pos3 = positions.reshape(B, 1, S)
    per_layer = (pre_mix_norm, w_in, w_a2, b_a2, gla_norm, w_o_gla, q_norm, w_uq, kv_norm, w_ukv, w_o_mla, w_gate,
                 b_gate, w_out, post_mix_norm, pre_ffn_norm, w_ffn_gate, w_ffn_up, w_ffn_down, post_ffn_norm)
    for l in range(pre_mix_norm.shape[0]):
        x2 = _layer(x2, pos3, B, S, *[p[l] for p in per_layer])
    return x2.reshape(B, S, D)
```

```python
import functools

import jax
import jax.numpy as jnp
from jax import lax
from jax.experimental import pallas as pl
from jax.experimental.pallas import tpu as pltpu

CHUNK = 64
GLA_HEADS = 4
GLA_DK = 128
GLA_DV = 256
GLA_LOWRANK = 16
GLA_GATE_TEMP = 16.0
MLA_HEADS = 8
MLA_Q_LORA = 384
MLA_KV_LORA = 256
MLA_NOPE = 128
MLA_ROPE = 64
MLA_V = 128
MLA_QK = MLA_NOPE + MLA_ROPE
ROPE_THETA = 10000.0
NORM_EPS = 1e-6

GLA_KW = GLA_HEADS * GLA_DK
GLA_VW = GLA_HEADS * GLA_DV
MLA_W = MLA_HEADS * MLA_V
SMALL_W = 768
KPE_OFF = MLA_Q_LORA + MLA_KV_LORA
HA_OFF = KPE_OFF + MLA_ROPE
LANES = 128
K_AUG = 256
V_AUG = MLA_V + 16
MASK_BIAS = float(jnp.finfo(jnp.bfloat16).min)
LOG2E = 1.4426950408889634

VMEM_LIMIT_BYTES = 56 * 1024 * 1024

BF16 = jnp.bfloat16
F32 = jnp.float32

_NT = (((1,), (1,)), ((), ()))


def _dot(a, b):
    return jnp.dot(a, b, preferred_element_type=F32)


def _dot_nt(a, b):
    return lax.dot_general(a, b, _NT, preferred_element_type=F32)


def _rms(x, gain):
    ms = jnp.mean(x * x, axis=-1, keepdims=True)
    return x * lax.rsqrt(ms + NORM_EPS) * gain


def _sigmoid(z):
    return 1.0 / (1.0 + jnp.exp(-z))


def _const_spec(shape):
    nd = len(shape)
    return pl.BlockSpec(shape, lambda *_: (0,) * nd, pipeline_mode=pl.Buffered(1))


def _params(sem, flags=None):
    return pltpu.CompilerParams(dimension_semantics=sem, vmem_limit_bytes=VMEM_LIMIT_BYTES, flags=flags)


def _in_proj_kernel(x_ref, pos_ref, gn_ref, wq_ref, wkT_ref, wv_ref, wg_ref, ws_ref, wa2T_ref, ba2_ref,
                    invf_ref, qn_ref, wuqT_ref, kvn_ref, wuk_ref, wuvT_ref,
                    q_ref, kT_ref, v_ref, g_ref, laT_ref, qT_ref, k_ref, vT_ref, *, tm, tk, tiles_per_q):
    h = _rms(x_ref[...], gn_ref[...]).astype(BF16)
    small = _dot(h, ws_ref[...])
    q_ref[...] = (_dot(h, wq_ref[...]) * (GLA_DK ** -0.5)).astype(BF16)
    kT_ref[...] = _dot_nt(wkT_ref[...], h).astype(BF16)
    v_ref[...] = _dot(h, wv_ref[...]).astype(BF16)
    g_ref[...] = _dot(h, wg_ref[...]).astype(BF16)
    last = small[:, KPE_OFF:]
    zT = _dot_nt(wa2T_ref[...], last.astype(BF16)) + ba2_ref[...]
    log_sig = jnp.minimum(zT, 0.0) - jnp.log(1.0 + jnp.exp(-jnp.abs(zT)))
    laT_ref[...] = log_sig * (1.0 / GLA_GATE_TEMP)

    cq = _rms(small[:, :MLA_Q_LORA], qn_ref[...]).astype(BF16)
    ckv = _rms(small[:, MLA_Q_LORA:KPE_OFF], kvn_ref[...]).astype(BF16)
    qT = _dot_nt(wuqT_ref[...], cq)
    k_nope = _dot(ckv, wuk_ref[...]).astype(BF16)
    vT = _dot_nt(wuvT_ref[...], ckv).astype(BF16)

    half = MLA_ROPE // 2
    ang = invf_ref[...] * pos_ref[...].astype(F32)
    cos = jnp.cos(ang)
    sin = jnp.sin(ang)

    scale = MLA_QK ** -0.5 * LOG2E
    for hd in range(MLA_HEADS):
        b0 = hd * MLA_QK
        qT_ref[b0:b0 + MLA_NOPE, :] = (qT[b0:b0 + MLA_NOPE] * scale).astype(BF16)
        x1 = qT[b0 + MLA_NOPE:b0 + MLA_NOPE + half]
        x2 = qT[b0 + MLA_NOPE + half:b0 + MLA_QK]
        qT_ref[b0 + MLA_NOPE:b0 + MLA_NOPE + half, :] = ((x1 * cos - x2 * sin) * scale).astype(BF16)
        qT_ref[b0 + MLA_NOPE + half:b0 + MLA_QK, :] = ((x2 * cos + x1 * sin) * scale).astype(BF16)

    zeros = jnp.zeros((LANES - MLA_ROPE, tm), F32)
    c_tab = jnp.concatenate([cos, cos, zeros], axis=0).T
    s_tab = jnp.concatenate([-sin, sin, zeros], axis=0).T
    lane = lax.broadcasted_iota(jnp.int32, last.shape, 1)
    swapped = jnp.where(lane < half, pltpu.roll(last, LANES - half, axis=1), pltpu.roll(last, half, axis=1))
    kpe = (last * c_tab + swapped * s_tab)[:, :MLA_ROPE].astype(BF16)

    ones = jnp.ones((V_AUG - MLA_V, tk), BF16)
    e = lax.broadcasted_iota(jnp.int32, (tk, K_AUG - MLA_QK), 1)
    chunk = lax.broadcasted_iota(jnp.int32, (tk, K_AUG - MLA_QK), 0) // CHUNK
    for s in range(tm // tk):
        rows = slice(s * tk, (s + 1) * tk)
        key_tile = pl.program_id(1) * (tm // tk) + s
        onehot = (e == chunk + (key_tile % tiles_per_q) * (tk // CHUNK)).astype(BF16)
        for hd in range(MLA_HEADS):
            k_ref[hd, s, :, :MLA_NOPE] = k_nope[rows, hd * MLA_NOPE:(hd + 1) * MLA_NOPE]
            k_ref[hd, s, :, MLA_NOPE:MLA_QK] = kpe[rows]
            k_ref[hd, s, :, MLA_QK:] = onehot
            vT_ref[hd, s, :MLA_V, :] = vT[hd * MLA_V:(hd + 1) * MLA_V, rows]
            vT_ref[hd, s, MLA_V:, :] = ones


def _in_proj(x2, pos3, gn, wq, wkT, wv, wg, ws, wa2T, ba2, invf, qn, wuqT, kvn, wuk, wuvT, B, S, tm, tk, tiles_per_q):
    T, D = x2.shape
    nt = S // tm
    sub = tm // tk
    row = lambda b, t: (b * nt + t, 0)
    col = lambda b, t: (0, b * nt + t)
    consts = (gn, wq, wkT, wv, wg, ws, wa2T, ba2, invf, qn, wuqT, kvn, wuk, wuvT)
    return pl.pallas_call(
        functools.partial(_in_proj_kernel, tm=tm, tk=tk, tiles_per_q=tiles_per_q),
        grid=(B, nt),
        in_specs=[pl.BlockSpec((tm, D), row), pl.BlockSpec((None, 1, tm), lambda b, t: (b, 0, t))]
        + [_const_spec(c.shape) for c in consts],
        out_specs=[
            pl.BlockSpec((tm, GLA_KW), row),
            pl.BlockSpec((GLA_KW, tm), col),
            pl.BlockSpec((tm, GLA_VW), row),
            pl.BlockSpec((tm, GLA_VW), row),
            pl.BlockSpec((GLA_KW, tm), col),
            pl.BlockSpec((None, MLA_HEADS * MLA_QK, tm), lambda b, t: (b, 0, t)),
            pl.BlockSpec((None, MLA_HEADS, sub, tk, K_AUG), lambda b, t: (b, 0, t, 0, 0)),
            pl.BlockSpec((None, MLA_HEADS, sub, V_AUG, tk), lambda b, t: (b, 0, t, 0, 0)),
        ],
        out_shape=[
            jax.ShapeDtypeStruct((T, GLA_KW), BF16),
            jax.ShapeDtypeStruct((GLA_KW, T), BF16),
            jax.ShapeDtypeStruct((T, GLA_VW), BF16),
            jax.ShapeDtypeStruct((T, GLA_VW), BF16),
            jax.ShapeDtypeStruct((GLA_KW, T), F32),
            jax.ShapeDtypeStruct((B, MLA_HEADS * MLA_QK, S), BF16),
            jax.ShapeDtypeStruct((B, MLA_HEADS, S // tk, tk, K_AUG), BF16),
            jax.ShapeDtypeStruct((B, MLA_HEADS, S // tk, V_AUG, tk), BF16),
        ],
        compiler_params=_params(("parallel", "parallel")),
        name="in_proj",
    )(x2, pos3, *consts)


PAIR = 2 * CHUNK


def _gla_kernel(q_ref, kT_ref, laT_ref, v_ref, g_ref, gn_ref, o_ref, state_ref, *, ts):
    @pl.when(pl.program_id(1) == 0)
    def _():
        state_ref[...] = jnp.zeros_like(state_ref)

    row = lax.broadcasted_iota(jnp.int32, (PAIR, PAIR), 0)
    col = lax.broadcasted_iota(jnp.int32, (PAIR, PAIR), 1)
    later = ((row // CHUNK == col // CHUNK) & (row > col)).astype(BF16)
    lane = lax.broadcasted_iota(jnp.int32, (GLA_DK, PAIR), 1)
    first = lane < CHUNK
    gn = gn_ref[...]

    units = [(p, h) for p in range(ts // PAIR) for h in range(GLA_HEADS)]
    tok = lambda p: slice(p * PAIR, (p + 1) * PAIR)
    ks = lambda h: slice(h * GLA_DK, (h + 1) * GLA_DK)
    vs = lambda h: slice(h * GLA_DV, (h + 1) * GLA_DV)

    to_end = {}
    for p, h in units:
        la = laT_ref[ks(h), tok(p)]
        la_hi = la.astype(BF16)
        la_lo = (la - la_hi.astype(F32)).astype(BF16)
        to_end[p, h] = _dot(la_hi, later) + _dot(la_lo, later)

    kv, dec = {}, {}
    for p, h in units:
        la = laT_ref[ks(h), tok(p)]
        kd = kT_ref[ks(h), tok(p)].astype(F32) * jnp.exp(to_end[p, h])
        v = v_ref[tok(p), vs(h)]
        kv[p, 0, h] = _dot(jnp.where(first, kd, 0.0).astype(BF16), v)
        kv[p, 1, h] = _dot(jnp.where(first, 0.0, kd).astype(BF16), v)
        dec[p, 0, h] = jnp.exp(jnp.sum(jnp.where(first, la, 0.0), axis=1, keepdims=True))
        dec[p, 1, h] = jnp.exp(jnp.sum(jnp.where(first, 0.0, la), axis=1, keepdims=True))

    states = [state_ref[h] for h in range(GLA_HEADS)]
    out = {}
    for p in range(ts // PAIR):
        for c in range(2):
            rows = slice(p * PAIR + c * CHUNK, p * PAIR + (c + 1) * CHUNK)
            for h in range(GLA_HEADS):
                states[h] = dec[p, c, h] * states[h] + kv[p, c, h]
                out[p, c, h] = _dot(q_ref[rows, ks(h)], states[h].astype(BF16))
    for h in range(GLA_HEADS):
        state_ref[h] = states[h]

    for p, h in units:
        o = _rms(jnp.concatenate([out[p, 0, h], out[p, 1, h]], axis=0), gn)
        g = g_ref[tok(p), vs(h)].astype(F32)
        o_ref[tok(p), vs(h)] = (o * (g * _sigmoid(g))).astype(BF16)


def _gla(q, kT, laT, v, g, gn, B, S, ts):
    T = B * S
    nt = S // ts
    row = lambda b, t: (b * nt + t, 0)
    col = lambda b, t: (0, b * nt + t)
    return pl.pallas_call(
        functools.partial(_gla_kernel, ts=ts),
        grid=(B, nt),
        in_specs=[
            pl.BlockSpec((ts, GLA_KW), row),
            pl.BlockSpec((GLA_KW, ts), col),
            pl.BlockSpec((GLA_KW, ts), col),
            pl.BlockSpec((ts, GLA_VW), row),
            pl.BlockSpec((ts, GLA_VW), row),
            _const_spec(gn.shape),
        ],
        out_specs=pl.BlockSpec((ts, GLA_VW), row),
        out_shape=jax.ShapeDtypeStruct((T, GLA_VW), BF16),
        scratch_shapes=[pltpu.VMEM((GLA_HEADS, GLA_DK, GLA_DV), F32)],
        compiler_params=_params(("parallel", "arbitrary")),
        name="gla",
    )(q, kT, laT, v, g, gn)


def _colmax(x):
    while x.shape[0] > 8 and x.shape[0] % 16 == 0:
        half = x.shape[0] // 2
        x = jnp.maximum(x[:half], x[half:])
    return jnp.max(x, axis=0, keepdims=True)


def _mla_attn_kernel(qT_ref, k_ref, vT_ref, o_ref, qa_ref, acc_ref, m_ref, s0_ref, s1_ref, x0_ref, x1_ref,
                     p0_ref, p1_ref, a0_ref, a1_ref, *, tq, tk):
    i = pl.program_id(2)
    r = tq // tk
    n = i * r
    s_refs, x_refs, p_refs, a_refs = (s0_ref, s1_ref), (x0_ref, x1_ref), (p0_ref, p1_ref), (a0_ref, a1_ref)

    qT = qT_ref[...]
    e = lax.broadcasted_iota(jnp.int32, (K_AUG - MLA_QK, tq), 0)
    q_chunk = lax.broadcasted_iota(jnp.int32, (K_AUG - MLA_QK, tq), 1) // CHUNK
    qa_ref[0, :MLA_QK, :] = qT
    qa_ref[0, MLA_QK:, :] = jnp.zeros((K_AUG - MLA_QK, tq), BF16)
    qa_ref[1, :MLA_QK, :] = qT
    qa_ref[1, MLA_QK:, :] = jnp.where(e > q_chunk, MASK_BIAS, 0.0).astype(BF16)

    acc_ref[...] = jnp.zeros_like(acc_ref)
    m_ref[...] = jnp.full_like(m_ref, -jnp.inf)
    p1_ref[...] = jnp.zeros_like(p1_ref)
    a1_ref[...] = jnp.ones_like(a1_ref)
    s_first = _dot(k_ref[0], qa_ref[(n == 0).astype(jnp.int32)])
    s0_ref[...] = s_first
    x0_ref[...] = _colmax(s_first)

    def stage(j, slot, next_sel, c_prev, c_cur, c_next):
        s_cur, x_cur, p_cur, a_cur = s_refs[slot], x_refs[slot], p_refs[slot], a_refs[slot]
        s_nxt, x_nxt, p_prv, a_prv = s_refs[1 - slot], x_refs[1 - slot], p_refs[1 - slot], a_refs[1 - slot]
        for c in range(r):
            cols = slice(c * tk, (c + 1) * tk)
            if c_next is not None and c * tk >= c_next:
                sT = _dot(k_ref[j + 1], qa_ref[next_sel, :, cols])
                s_nxt[:, cols] = sT
                x_nxt[:, cols] = _colmax(sT)
            if c_prev is not None and c * tk >= c_prev:
                pv = _dot(vT_ref[jnp.maximum(j - 1, 0)], p_prv[:, cols])
                acc_ref[:, cols] = a_prv[:, cols] * acc_ref[:, cols] + pv
            if c_cur is not None and c * tk >= c_cur:
                m_old = m_ref[:, cols]
                m_new = jnp.maximum(m_old, x_cur[:, cols])
                a_cur[:, cols] = jnp.exp2(m_old - m_new)
                p_cur[:, cols] = jnp.exp2(s_cur[:, cols] - m_new).astype(BF16)
                m_ref[:, cols] = m_new

    def body(jj, carry):
        for u in range(r):
            j = r * jj + u
            stage(j, u % 2, (j + 1 == n).astype(jnp.int32), 0, 0, 0)
        return carry

    lax.fori_loop(0, i, body, 0)
    for d in range(r):
        stage(n + d, d % 2, 1, max(d - 1, 0) * tk, d * tk, (d + 1) * tk if d + 1 < r else None)
    stage(n + r, r % 2, 1, (r - 1) * tk, None, None)
    o_ref[...] = (acc_ref[:MLA_V, :] / acc_ref[MLA_V:MLA_V + 1, :]).T.astype(o_ref.dtype)


def _mla_attn(qT, k5, vT5, B, S, tq):
    nk, tk = k5.shape[2], k5.shape[3]
    assert (tq // tk) % 2 == 0 and tq // CHUNK <= K_AUG - MLA_QK
    return pl.pallas_call(
        functools.partial(_mla_attn_kernel, tq=tq, tk=tk),
        grid=(B, MLA_HEADS, S // tq),
        in_specs=[
            pl.BlockSpec((None, MLA_QK, tq), lambda b, h, i: (b, h, i)),
            pl.BlockSpec((None, None, nk, tk, K_AUG), lambda b, h, i: (b, h, 0, 0, 0)),
            pl.BlockSpec((None, None, nk, V_AUG, tk), lambda b, h, i: (b, h, 0, 0, 0)),
        ],
        out_specs=pl.BlockSpec((None, tq, MLA_V), lambda b, h, i: (b, i, h)),
        out_shape=jax.ShapeDtypeStruct((B, S, MLA_W), BF16),
        scratch_shapes=[
            pltpu.VMEM((2, K_AUG, tq), BF16), pltpu.VMEM((V_AUG, tq), F32), pltpu.VMEM((1, tq), F32),
            pltpu.VMEM((tk, tq), F32), pltpu.VMEM((tk, tq), F32),
            pltpu.VMEM((1, tq), F32), pltpu.VMEM((1, tq), F32),
            pltpu.VMEM((tk, tq), BF16), pltpu.VMEM((tk, tq), BF16),
            pltpu.VMEM((1, tq), F32), pltpu.VMEM((1, tq), F32),
        ],
        compiler_params=_params(("parallel", "parallel", "arbitrary")),
        name="mla_attn",
    )(qT, k5, vT5)


def _mix_kernel(x_ref, a_ref, b_ref, gn_ref, wgate_ref, bgate_ref, woa_ref, wob_ref, wout_ref, pn_ref, o_ref):
    x = x_ref[...]
    d = x.shape[-1]
    h = _rms(x, gn_ref[...]).astype(BF16)
    gates = _sigmoid(_dot(h, wgate_ref[...]) + bgate_ref[...])
    y_a = _dot(a_ref[...], woa_ref[...])
    y_b = _dot(b_ref[...], wob_ref[...])
    mixed = (gates[:, :d] * y_a + gates[:, d:] * y_b).astype(BF16)
    o_ref[...] = x + _rms(_dot(mixed, wout_ref[...]), pn_ref[...])


def _mix(x2, a, b, gn, wgate, bgate, woa, wob, wout, pn, tm):
    T, D = x2.shape
    row = lambda i: (i, 0)
    return pl.pallas_call(
        _mix_kernel,
        grid=(T // tm,),
        in_specs=[
            pl.BlockSpec((tm, D), row), pl.BlockSpec((tm, a.shape[1]), row), pl.BlockSpec((tm, b.shape[1]), row),
            _const_spec(gn.shape), _const_spec(wgate.shape), _const_spec(bgate.shape),
            _const_spec(woa.shape), _const_spec(wob.shape), _const_spec(wout.shape), _const_spec(pn.shape),
        ],
        out_specs=pl.BlockSpec((tm, D), row),
        out_shape=jax.ShapeDtypeStruct((T, D), F32),
        compiler_params=_params(("parallel",)),
        name="mix",
    )(x2, a, b, gn, wgate, bgate, woa, wob, wout, pn)


def _ffn_kernel(x_ref, gn_ref, wg_ref, wu_ref, wd_ref, pn_ref, o_ref):
    x = x_ref[...]
    h = _rms(x, gn_ref[...]).astype(BF16)
    gate = _dot(h, wg_ref[...])
    up = _dot(h, wu_ref[...])
    act = (gate * _sigmoid(gate) * up).astype(BF16)
    o_ref[...] = x + _rms(_dot(act, wd_ref[...]), pn_ref[...])


def _ffn(x2, gn, wg, wu, wd, pn, tm):
    T, D = x2.shape
    row = lambda i: (i, 0)
    return pl.pallas_call(
        _ffn_kernel,
        grid=(T // tm,),
        in_specs=[
            pl.BlockSpec((tm, D), row),
            _const_spec(gn.shape), _const_spec(wg.shape), _const_spec(wu.shape),
            _const_spec(wd.shape), _const_spec(pn.shape),
        ],
        out_specs=pl.BlockSpec((tm, D), row),
        out_shape=jax.ShapeDtypeStruct((T, D), F32),
        compiler_params=_params(("parallel",)),
        name="ffn",
    )(x2, gn, wg, wu, wd, pn)


def _tile(n, pref):
    t = min(n, pref)
    assert n % t == 0, (n, t)
    return t


def _layer(x2, pos3, B, S, pre_mix_norm, w_in, w_a2, b_a2, gla_norm, w_o_gla, q_norm, w_uq, kv_norm, w_ukv,
           w_o_mla, w_gate, b_gate, w_out, post_mix_norm, pre_ffn_norm, w_ffn_gate, w_ffn_up, w_ffn_down,
           post_ffn_norm):
    T, D = x2.shape
    bf = lambda w: w.astype(BF16)
    r2 = lambda v: v.reshape(1, -1)

    o = 0
    wq = bf(w_in[:, o:o + GLA_KW]); o += GLA_KW
    wkT = bf(w_in[:, o:o + GLA_KW]).T; o += GLA_KW
    wv = bf(w_in[:, o:o + GLA_VW]); o += GLA_VW
    wg = bf(w_in[:, o:o + GLA_VW]); o += GLA_VW
    w_ha = w_in[:, o:o + GLA_LOWRANK]; o += GLA_LOWRANK
    w_mla = w_in[:, o:]
    pad = SMALL_W - HA_OFF - GLA_LOWRANK
    ws = bf(jnp.concatenate([w_mla, w_ha, jnp.zeros((D, pad), w_in.dtype)], axis=1))
    wa2T = jnp.zeros((GLA_KW, LANES), F32).at[:, MLA_ROPE:MLA_ROPE + GLA_LOWRANK].set(w_a2.T)
    wa2T = bf(wa2T)
    ba2 = b_a2.reshape(-1, 1)

    wuqT = bf(w_uq).T
    w_ukv3 = w_ukv.reshape(MLA_KV_LORA, MLA_HEADS, MLA_NOPE + MLA_V)
    wuk = bf(w_ukv3[:, :, :MLA_NOPE].reshape(MLA_KV_LORA, MLA_HEADS * MLA_NOPE))
    wuvT = bf(w_ukv3[:, :, MLA_NOPE:].reshape(MLA_KV_LORA, MLA_W)).T
    invf = (1.0 / (ROPE_THETA ** (jnp.arange(0, MLA_ROPE, 2, dtype=F32) / MLA_ROPE))).reshape(-1, 1)

    tm = _tile(T, 512)
    tk = _tile(S, 256)
    tq = _tile(S, 2048)
    q, kT, v, g, laT, qT, k5, vT5 = _in_proj(
        x2, pos3, r2(pre_mix_norm), wq, wkT, wv, wg, ws, wa2T, ba2, invf, r2(q_norm), wuqT, r2(kv_norm), wuk, wuvT,
        B, S, _tile(S, 512), tk, tq // tk)
    gla_o = _gla(q, kT, laT, v, g, r2(gla_norm), B, S, _tile(S, 256))
    mla_o = _mla_attn(qT, k5, vT5, B, S, tq).reshape(T, MLA_W)

    x1 = _mix(x2, gla_o, mla_o, r2(pre_mix_norm), bf(w_gate), r2(b_gate), bf(w_o_gla), bf(w_o_mla), bf(w_out),
              r2(post_mix_norm), tm)
    return _ffn(x1, r2(pre_ffn_norm), bf(w_ffn_gate), bf(w_ffn_up), bf(w_ffn_down), r2(post_ffn_norm), tm)


def kernel(x, positions, pre_mix_norm, w_in, w_a2, b_a2, gla_norm, w_o_gla, q_norm, w_uq, kv_norm, w_ukv, w_o_mla,
           w_gate, b_gate, w_out, post_mix_norm, pre_ffn_norm, w_ffn_gate, w_ffn_up, w_ffn_down, post_ffn_norm):
    B, S, D = x.shape
    x2 = x.reshape(B * S, D)
    pos3 = positions.reshape(B, 1, S)
    per_layer = (pre_mix_norm, w_in, w_a2, b_a2, gla_norm, w_o_gla, q_norm, w_uq, kv_norm, w_ukv, w_o_mla, w_gate,
                 b_gate, w_out, post_mix_norm, pre_ffn_norm, w_ffn_gate, w_ffn_up, w_ffn_down, post_ffn_norm)
    for l in range(pre_mix_norm.shape[0]):
        x2 = _layer(x2, pos3, B, S, *[p[l] for p in per_layer])
    return x2.reshape(B, S, D)
```

```python
import functools

import jax
import jax.numpy as jnp
from jax import lax
from jax.experimental import pallas as pl
from jax.experimental.pallas import tpu as pltpu

CHUNK = 64
GLA_HEADS = 4
GLA_DK = 128
GLA_DV = 256
GLA_LOWRANK = 16
GLA_GATE_TEMP = 16.0
MLA_HEADS = 8
MLA_Q_LORA = 384
MLA_KV_LORA = 256
MLA_NOPE = 128
MLA_ROPE = 64
MLA_V = 128
MLA_QK = MLA_NOPE + MLA_ROPE
ROPE_THETA = 10000.0
NORM_EPS = 1e-6

GLA_KW = GLA_HEADS * GLA_DK
GLA_VW = GLA_HEADS * GLA_DV
MLA_W = MLA_HEADS * MLA_V
SMALL_W = 768
KPE_OFF = MLA_Q_LORA + MLA_KV_LORA
HA_OFF = KPE_OFF + MLA_ROPE
LANES = 128
K_AUG = 256
V_AUG = MLA_V + 16
MASK_BIAS = float(jnp.finfo(jnp.bfloat16).min)
LOG2E = 1.4426950408889634

VMEM_LIMIT_BYTES = 56 * 1024 * 1024

BF16 = jnp.bfloat16
F32 = jnp.float32

_NT = (((1,), (1,)), ((), ()))


def _dot(a, b):
    return jnp.dot(a, b, preferred_element_type=F32)


def _dot_nt(a, b):
    return lax.dot_general(a, b, _NT, preferred_element_type=F32)


def _rms(x, gain):
    ms = jnp.mean(x * x, axis=-1, keepdims=True)
    return x * lax.rsqrt(ms + NORM_EPS) * gain


def _sigmoid(z):
    return 1.0 / (1.0 + jnp.exp(-z))


def _const_spec(shape):
    nd = len(shape)
    return pl.BlockSpec(shape, lambda *_: (0,) * nd, pipeline_mode=pl.Buffered(1))


def _params(sem, flags=None):
    return pltpu.CompilerParams(dimension_semantics=sem, vmem_limit_bytes=VMEM_LIMIT_BYTES, flags=flags)


def _in_proj_kernel(x_ref, pos_ref, gn_ref, wq_ref, wkT_ref, wv_ref, wg_ref, ws_ref, wa2T_ref, ba2_ref,
                    invf_ref, qn_ref, wuqT_ref, kvn_ref, wuk_ref, wuvT_ref,
                    q_ref, kT_ref, v_ref, g_ref, laT_ref, qT_ref, k_ref, vT_ref, *, tm, tk, tiles_per_q):
    h = _rms(x_ref[...], gn_ref[...]).astype(BF16)
    small = _dot(h, ws_ref[...])
    q_ref[...] = (_dot(h, wq_ref[...]) * (GLA_DK ** -0.5)).astype(BF16)
    kT_ref[...] = _dot_nt(wkT_ref[...], h).astype(BF16)
    v_ref[...] = _dot(h, wv_ref[...]).astype(BF16)
    g_ref[...] = _dot(h, wg_ref[...]).astype(BF16)
    last = small[:, KPE_OFF:]
    zT = _dot_nt(wa2T_ref[...], last.astype(BF16)) + ba2_ref[...]
    log_sig = jnp.minimum(zT, 0.0) - jnp.log(1.0 + jnp.exp(-jnp.abs(zT)))
    laT_ref[...] = log_sig * (1.0 / GLA_GATE_TEMP)

    cq = _rms(small[:, :MLA_Q_LORA], qn_ref[...]).astype(BF16)
    ckv = _rms(small[:, MLA_Q_LORA:KPE_OFF], kvn_ref[...]).astype(BF16)
    qT = _dot_nt(wuqT_ref[...], cq)
    k_nope = _dot(ckv, wuk_ref[...]).astype(BF16)
    vT = _dot_nt(wuvT_ref[...], ckv).astype(BF16)

    half = MLA_ROPE // 2
    ang = invf_ref[...] * pos_ref[...].astype(F32)
    cos = jnp.cos(ang)
    sin = jnp.sin(ang)

    scale = MLA_QK ** -0.5 * LOG2E
    for hd in range(MLA_HEADS):
        b0 = hd * MLA_QK
        qT_ref[b0:b0 + MLA_NOPE, :] = (qT[b0:b0 + MLA_NOPE] * scale).astype(BF16)
        x1 = qT[b0 + MLA_NOPE:b0 + MLA_NOPE + half]
        x2 = qT[b0 + MLA_NOPE + half:b0 + MLA_QK]
        qT_ref[b0 + MLA_NOPE:b0 + MLA_NOPE + half, :] = ((x1 * cos - x2 * sin) * scale).astype(BF16)
        qT_ref[b0 + MLA_NOPE + half:b0 + MLA_QK, :] = ((x2 * cos + x1 * sin) * scale).astype(BF16)

    zeros = jnp.zeros((LANES - MLA_ROPE, tm), F32)
    c_tab = jnp.concatenate([cos, cos, zeros], axis=0).T
    s_tab = jnp.concatenate([-sin, sin, zeros], axis=0).T
    lane = lax.broadcasted_iota(jnp.int32, last.shape, 1)
    swapped = jnp.where(lane < half, pltpu.roll(last, LANES - half, axis=1), pltpu.roll(last, half, axis=1))
    kpe = (last * c_tab + swapped * s_tab)[:, :MLA_ROPE].astype(BF16)

    ones = jnp.ones((V_AUG - MLA_V, tk), BF16)
    e = lax.broadcasted_iota(jnp.int32, (tk, K_AUG - MLA_QK), 1)
    chunk = lax.broadcasted_iota(jnp.int32, (tk, K_AUG - MLA_QK), 0) // CHUNK
    for s in range(tm // tk):
        rows = slice(s * tk, (s + 1) * tk)
        key_tile = pl.program_id(1) * (tm // tk) + s
        onehot = (e == chunk + (key_tile % tiles_per_q) * (tk // CHUNK)).astype(BF16)
        for hd in range(MLA_HEADS):
            k_ref[hd, s, :, :MLA_NOPE] = k_nope[rows, hd * MLA_NOPE:(hd + 1) * MLA_NOPE]
            k_ref[hd, s, :, MLA_NOPE:MLA_QK] = kpe[rows]
            k_ref[hd, s, :, MLA_QK:] = onehot
            vT_ref[hd, s, :MLA_V, :] = vT[hd * MLA_V:(hd + 1) * MLA_V, rows]
            vT_ref[hd, s, MLA_V:, :] = ones


def _in_proj(x2, pos3, gn, wq, wkT, wv, wg, ws, wa2T, ba2, invf, qn, wuqT, kvn, wuk, wuvT, B, S, tm, tk, tiles_per_q):
    T, D = x2.shape
    nt = S // tm
    sub = tm // tk
    row = lambda b, t: (b * nt + t, 0)
    col = lambda b, t: (0, b * nt + t)
    consts = (gn, wq, wkT, wv, wg, ws, wa2T, ba2, invf, qn, wuqT, kvn, wuk, wuvT)
    return pl.pallas_call(
        functools.partial(_in_proj_kernel, tm=tm, tk=tk, tiles_per_q=tiles_per_q),
        grid=(B, nt),
        in_specs=[pl.BlockSpec((tm, D), row), pl.BlockSpec((None, 1, tm), lambda b, t: (b, 0, t))]
        + [_const_spec(c.shape) for c in consts],
        out_specs=[
            pl.BlockSpec((tm, GLA_KW), row),
            pl.BlockSpec((GLA_KW, tm), col),
            pl.BlockSpec((tm, GLA_VW), row),
            pl.BlockSpec((tm, GLA_VW), row),
            pl.BlockSpec((GLA_KW, tm), col),
            pl.BlockSpec((None, MLA_HEADS * MLA_QK, tm), lambda b, t: (b, 0, t)),
            pl.BlockSpec((None, MLA_HEADS, sub, tk, K_AUG), lambda b, t: (b, 0, t, 0, 0)),
            pl.BlockSpec((None, MLA_HEADS, sub, V_AUG, tk), lambda b, t: (b, 0, t, 0, 0)),
        ],
        out_shape=[
            jax.ShapeDtypeStruct((T, GLA_KW), BF16),
            jax.ShapeDtypeStruct((GLA_KW, T), BF16),
            jax.ShapeDtypeStruct((T, GLA_VW), BF16),
            jax.ShapeDtypeStruct((T, GLA_VW), BF16),
            jax.ShapeDtypeStruct((GLA_KW, T), F32),
            jax.ShapeDtypeStruct((B, MLA_HEADS * MLA_QK, S), BF16),
            jax.ShapeDtypeStruct((B, MLA_HEADS, S // tk, tk, K_AUG), BF16),
            jax.ShapeDtypeStruct((B, MLA_HEADS, S // tk, V_AUG, tk), BF16),
        ],
        compiler_params=_params(("parallel", "parallel")),
        name="in_proj",
    )(x2, pos3, *consts)


PAIR = 2 * CHUNK


def _gla_kernel(q_ref, kT_ref, laT_ref, v_ref, g_ref, gn_ref, o_ref, state_ref, *, ts):
    @pl.when(pl.program_id(1) == 0)
    def _():
        state_ref[...] = jnp.zeros_like(state_ref)

    row = lax.broadcasted_iota(jnp.int32, (PAIR, PAIR), 0)
    col = lax.broadcasted_iota(jnp.int32, (PAIR, PAIR), 1)
    later = ((row // CHUNK == col // CHUNK) & (row > col)).astype(BF16)
    lane = lax.broadcasted_iota(jnp.int32, (GLA_DK, PAIR), 1)
    first = lane < CHUNK
    gn = gn_ref[...]

    units = [(p, h) for p in range(ts // PAIR) for h in range(GLA_HEADS)]
    tok = lambda p: slice(p * PAIR, (p + 1) * PAIR)
    ks = lambda h: slice(h * GLA_DK, (h + 1) * GLA_DK)
    vs = lambda h: slice(h * GLA_DV, (h + 1) * GLA_DV)

    to_end = {}
    for p, h in units:
        la = laT_ref[ks(h), tok(p)]
        la_hi = la.astype(BF16)
        la_lo = (la - la_hi.astype(F32)).astype(BF16)
        to_end[p, h] = _dot(la_hi, later) + _dot(la_lo, later)

    kv, dec = {}, {}
    for p, h in units:
        la = laT_ref[ks(h), tok(p)]
        kd = kT_ref[ks(h), tok(p)].astype(F32) * jnp.exp(to_end[p, h])
        v = v_ref[tok(p), vs(h)]
        kv[p, 0, h] = _dot(jnp.where(first, kd, 0.0).astype(BF16), v)
        kv[p, 1, h] = _dot(jnp.where(first, 0.0, kd).astype(BF16), v)
        dec[p, 0, h] = jnp.exp(jnp.sum(jnp.where(first, la, 0.0), axis=1, keepdims=True))
        dec[p, 1, h] = jnp.exp(jnp.sum(jnp.where(first, 0.0, la), axis=1, keepdims=True))

    states = [state_ref[h] for h in range(GLA_HEADS)]
    out = {}
    for p in range(ts // PAIR):
        for c in range(2):
            rows = slice(p * PAIR + c * CHUNK, p * PAIR + (c + 1) * CHUNK)
            for h in range(GLA_HEADS):
                states[h] = dec[p, c, h] * states[h] + kv[p, c, h]
                out[p, c, h] = _dot(q_ref[rows, ks(h)], states[h].astype(BF16))
    for h in range(GLA_HEADS):
        state_ref[h] = states[h]

    for p, h in units:
        o = _rms(jnp.concatenate([out[p, 0, h], out[p, 1, h]], axis=0), gn)
        g = g_ref[tok(p), vs(h)].astype(F32)
        o_ref[tok(p), vs(h)] = (o * (g * _sigmoid(g))).astype(BF16)


def _gla(q, kT, laT, v, g, gn, B, S, ts):
    T = B * S
    nt = S // ts
    row = lambda b, t: (b * nt + t, 0)
    col = lambda b, t: (0, b * nt + t)
    return pl.pallas_call(
        functools.partial(_gla_kernel, ts=ts),
        grid=(B, nt),
        in_specs=[
            pl.BlockSpec((ts, GLA_KW), row),
            pl.BlockSpec((GLA_KW, ts), col),
            pl.BlockSpec((GLA_KW, ts), col),
            pl.BlockSpec((ts, GLA_VW), row),
            pl.BlockSpec((ts, GLA_VW), row),
            _const_spec(gn.shape),
        ],
        out_specs=pl.BlockSpec((ts, GLA_VW), row),
        out_shape=jax.ShapeDtypeStruct((T, GLA_VW), BF16),
        scratch_shapes=[pltpu.VMEM((GLA_HEADS, GLA_DK, GLA_DV), F32)],
        compiler_params=_params(("parallel", "arbitrary")),
        name="gla",
    )(q, kT, laT, v, g, gn)


def _colmax(x):
    while x.shape[0] > 8 and x.shape[0] % 16 == 0:
        half = x.shape[0] // 2
        x = jnp.maximum(x[:half], x[half:])
    return jnp.max(x, axis=0, keepdims=True)


ATTN_SCRATCH_PER_HEAD = 11
QGROUP = 256


def _mla_attn_kernel(qT_ref, k_ref, vT_ref, o_ref, *scratch, tq, tk, heads):
    i = pl.program_id(2)
    r = tq // tk
    n = i * r
    per_head = [scratch[hd * ATTN_SCRATCH_PER_HEAD:(hd + 1) * ATTN_SCRATCH_PER_HEAD] for hd in range(heads)]

    e = lax.broadcasted_iota(jnp.int32, (K_AUG - MLA_QK, tq), 0)
    q_chunk = lax.broadcasted_iota(jnp.int32, (K_AUG - MLA_QK, tq), 1) // CHUNK
    first_sel = (n == 0).astype(jnp.int32)
    for hd, (qa_ref, acc_ref, m_ref, s0_ref, s1_ref, x0_ref, x1_ref, p0_ref, p1_ref, a0_ref, a1_ref) in enumerate(per_head):
        qT = qT_ref[hd * MLA_QK:(hd + 1) * MLA_QK, :]
        qa_ref[0, :MLA_QK, :] = qT
        qa_ref[0, MLA_QK:, :] = jnp.zeros((K_AUG - MLA_QK, tq), BF16)
        qa_ref[1, :MLA_QK, :] = qT
        qa_ref[1, MLA_QK:, :] = jnp.where(e > q_chunk, MASK_BIAS, 0.0).astype(BF16)
        acc_ref[...] = jnp.zeros_like(acc_ref)
        m_ref[...] = jnp.full_like(m_ref, -jnp.inf)
        p1_ref[...] = jnp.zeros_like(p1_ref)
        a1_ref[...] = jnp.ones_like(a1_ref)
        s_first = _dot(k_ref[hd, 0], qa_ref[first_sel])
        s0_ref[...] = s_first
        x0_ref[...] = _colmax(s_first)

    def stage(j, slot, next_sel, c_prev, c_cur, c_next):
        for c in range(tq // QGROUP):
            cols = slice(c * QGROUP, (c + 1) * QGROUP)
            for hd, (qa_ref, acc_ref, m_ref, s0_ref, s1_ref, x0_ref, x1_ref, p0_ref, p1_ref, a0_ref, a1_ref) in enumerate(per_head):
                s_refs, x_refs, p_refs, a_refs = (s0_ref, s1_ref), (x0_ref, x1_ref), (p0_ref, p1_ref), (a0_ref, a1_ref)
                s_cur, x_cur, p_cur, a_cur = s_refs[slot], x_refs[slot], p_refs[slot], a_refs[slot]
                s_nxt, x_nxt, p_prv, a_prv = s_refs[1 - slot], x_refs[1 - slot], p_refs[1 - slot], a_refs[1 - slot]
                if c_next is not None and c * QGROUP >= c_next:
                    sT = _dot(k_ref[hd, j + 1], qa_ref[next_sel, :, cols])
                    s_nxt[:, cols] = sT
                    x_nxt[:, cols] = _colmax(sT)
                if c_prev is not None and c * QGROUP >= c_prev:
                    pv = _dot(vT_ref[hd, jnp.maximum(j - 1, 0)], p_prv[:, cols])
                    acc_ref[:, cols] = a_prv[:, cols] * acc_ref[:, cols] + pv
                if c_cur is not None and c * QGROUP >= c_cur:
                    m_old = m_ref[:, cols]
                    m_new = jnp.maximum(m_old, x_cur[:, cols])
                    a_cur[:, cols] = jnp.exp2(m_old - m_new)
                    p_cur[:, cols] = jnp.exp2((s_cur[:, cols] - m_new).astype(BF16))
                    m_ref[:, cols] = m_new

    def body(jj, carry):
        for u in range(r):
            j = r * jj + u
            stage(j, u % 2, (j + 1 == n).astype(jnp.int32), 0, 0, 0)
        return carry

    lax.fori_loop(0, i, body, 0)
    for d in range(r):
        stage(n + d, d % 2, 1, max(d - 1, 0) * tk, d * tk, (d + 1) * tk if d + 1 < r else None)
    stage(n + r, r % 2, 1, (r - 1) * tk, None, None)
    for hd, refs in enumerate(per_head):
        acc_ref = refs[1]
        o_ref[:, hd * MLA_V:(hd + 1) * MLA_V] = (acc_ref[:MLA_V, :] / acc_ref[MLA_V:MLA_V + 1, :]).T.astype(o_ref.dtype)


def _mla_attn(qT, k5, vT5, B, S, tq, heads):
    nk, tk = k5.shape[2], k5.shape[3]
    assert (tq // tk) % 2 == 0 and tq // CHUNK <= K_AUG - MLA_QK and MLA_HEADS % heads == 0
    per_head = [
        pltpu.VMEM((2, K_AUG, tq), BF16), pltpu.VMEM((V_AUG, tq), F32), pltpu.VMEM((1, tq), F32),
        pltpu.VMEM((tk, tq), F32), pltpu.VMEM((tk, tq), F32),
        pltpu.VMEM((1, tq), F32), pltpu.VMEM((1, tq), F32),
        pltpu.VMEM((tk, tq), BF16), pltpu.VMEM((tk, tq), BF16),
        pltpu.VMEM((1, tq), F32), pltpu.VMEM((1, tq), F32),
    ]
    assert len(per_head) == ATTN_SCRATCH_PER_HEAD
    return pl.pallas_call(
        functools.partial(_mla_attn_kernel, tq=tq, tk=tk, heads=heads),
        grid=(B, MLA_HEADS // heads, S // tq),
        in_specs=[
            pl.BlockSpec((None, heads * MLA_QK, tq), lambda b, h, i: (b, h, i)),
            pl.BlockSpec((None, heads, nk, tk, K_AUG), lambda b, h, i: (b, h, 0, 0, 0)),
            pl.BlockSpec((None, heads, nk, V_AUG, tk), lambda b, h, i: (b, h, 0, 0, 0)),
        ],
        out_specs=pl.BlockSpec((None, tq, heads * MLA_V), lambda b, h, i: (b, i, h)),
        out_shape=jax.ShapeDtypeStruct((B, S, MLA_W), BF16),
        scratch_shapes=per_head * heads,
        compiler_params=_params(("parallel", "parallel", "arbitrary")),
        name="mla_attn",
    )(qT, k5, vT5)


def _mix_kernel(x_ref, a_ref, b_ref, gn_ref, wgate_ref, bgate_ref, woa_ref, wob_ref, wout_ref, pn_ref, o_ref):
    x = x_ref[...]
    d = x.shape[-1]
    h = _rms(x, gn_ref[...]).astype(BF16)
    gates = _sigmoid(_dot(h, wgate_ref[...]) + bgate_ref[...])
    y_a = _dot(a_ref[...], woa_ref[...])
    y_b = _dot(b_ref[...], wob_ref[...])
    mixed = (gates[:, :d] * y_a + gates[:, d:] * y_b).astype(BF16)
    o_ref[...] = x + _rms(_dot(mixed, wout_ref[...]), pn_ref[...])


def _mix(x2, a, b, gn, wgate, bgate, woa, wob, wout, pn, tm):
    T, D = x2.shape
    row = lambda i: (i, 0)
    return pl.pallas_call(
        _mix_kernel,
        grid=(T // tm,),
        in_specs=[
            pl.BlockSpec((tm, D), row), pl.BlockSpec((tm, a.shape[1]), row), pl.BlockSpec((tm, b.shape[1]), row),
            _const_spec(gn.shape), _const_spec(wgate.shape), _const_spec(bgate.shape),
            _const_spec(woa.shape), _const_spec(wob.shape), _const_spec(wout.shape), _const_spec(pn.shape),
        ],
        out_specs=pl.BlockSpec((tm, D), row),
        out_shape=jax.ShapeDtypeStruct((T, D), F32),
        compiler_params=_params(("parallel",)),
        name="mix",
    )(x2, a, b, gn, wgate, bgate, woa, wob, wout, pn)


def _ffn_kernel(x_ref, gn_ref, wg_ref, wu_ref, wd_ref, pn_ref, o_ref):
    x = x_ref[...]
    h = _rms(x, gn_ref[...]).astype(BF16)
    gate = _dot(h, wg_ref[...])
    up = _dot(h, wu_ref[...])
    act = (gate * _sigmoid(gate) * up).astype(BF16)
    o_ref[...] = x + _rms(_dot(act, wd_ref[...]), pn_ref[...])


def _ffn(x2, gn, wg, wu, wd, pn, tm):
    T, D = x2.shape
    row = lambda i: (i, 0)
    return pl.pallas_call(
        _ffn_kernel,
        grid=(T // tm,),
        in_specs=[
            pl.BlockSpec((tm, D), row),
            _const_spec(gn.shape), _const_spec(wg.shape), _const_spec(wu.shape),
            _const_spec(wd.shape), _const_spec(pn.shape),
        ],
        out_specs=pl.BlockSpec((tm, D), row),
        out_shape=jax.ShapeDtypeStruct((T, D), F32),
        compiler_params=_params(("parallel",)),
        name="ffn",
    )(x2, gn, wg, wu, wd, pn)


def _tile(n, pref):
    t = min(n, pref)
    assert n % t == 0, (n, t)
    return t


def _layer(x2, pos3, B, S, pre_mix_norm, w_in, w_a2, b_a2, gla_norm, w_o_gla, q_norm, w_uq, kv_norm, w_ukv,
           w_o_mla, w_gate, b_gate, w_out, post_mix_norm, pre_ffn_norm, w_ffn_gate, w_ffn_up, w_ffn_down,
           post_ffn_norm):
    T, D = x2.shape
    bf = lambda w: w.astype(BF16)
    r2 = lambda v: v.reshape(1, -1)

    o = 0
    wq = bf(w_in[:, o:o + GLA_KW]); o += GLA_KW
    wkT = bf(w_in[:, o:o + GLA_KW]).T; o += GLA_KW
    wv = bf(w_in[:, o:o + GLA_VW]); o += GLA_VW
    wg = bf(w_in[:, o:o + GLA_VW]); o += GLA_VW
    w_ha = w_in[:, o:o + GLA_LOWRANK]; o += GLA_LOWRANK
    w_mla = w_in[:, o:]
    pad = SMALL_W - HA_OFF - GLA_LOWRANK
    ws = bf(jnp.concatenate([w_mla, w_ha, jnp.zeros((D, pad), w_in.dtype)], axis=1))
    wa2T = jnp.zeros((GLA_KW, LANES), F32).at[:, MLA_ROPE:MLA_ROPE + GLA_LOWRANK].set(w_a2.T)
    wa2T = bf(wa2T)
    ba2 = b_a2.reshape(-1, 1)

    wuqT = bf(w_uq).T
    w_ukv3 = w_ukv.reshape(MLA_KV_LORA, MLA_HEADS, MLA_NOPE + MLA_V)
    wuk = bf(w_ukv3[:, :, :MLA_NOPE].reshape(MLA_KV_LORA, MLA_HEADS * MLA_NOPE))
    wuvT = bf(w_ukv3[:, :, MLA_NOPE:].reshape(MLA_KV_LORA, MLA_W)).T
    invf = (1.0 / (ROPE_THETA ** (jnp.arange(0, MLA_ROPE, 2, dtype=F32) / MLA_ROPE))).reshape(-1, 1)

    tm = _tile(T, 512)
    tk = _tile(S, 256)
    tq = _tile(S, 2048)
    q, kT, v, g, laT, qT, k5, vT5 = _in_proj(
        x2, pos3, r2(pre_mix_norm), wq, wkT, wv, wg, ws, wa2T, ba2, invf, r2(q_norm), wuqT, r2(kv_norm), wuk, wuvT,
        B, S, _tile(S, 512), tk, tq // tk)
    gla_o = _gla(q, kT, laT, v, g, r2(gla_norm), B, S, _tile(S, 256))
    mla_o = _mla_attn(qT, k5, vT5, B, S, tq, 1).reshape(T, MLA_W)

    x1 = _mix(x2, gla_o, mla_o, r2(pre_mix_norm), bf(w_gate), r2(b_gate), bf(w_o_gla), bf(w_o_mla), bf(w_out),
              r2(post_mix_norm), tm)
    return _ffn(x1, r2(pre_ffn_norm), bf(w_ffn_gate), bf(w_ffn_up), bf(w_ffn_down), r2(post_ffn_norm), tm)


def kernel(x, positions, pre_mix_norm, w_in, w_a2, b_a2, gla_norm, w_o_gla, q_norm, w_uq, kv_norm, w_ukv, w_o_mla,
           w_gate, b_gate, w_out, post_mix_norm, pre_ffn_norm, w_ffn_gate, w_ffn_up, w_ffn_down, post_ffn_norm):
    B, S, D = x.shape
    x2 = x.reshape(B * S, D)
    pos3 = positions.reshape(B, 1, S)
    per_layer = (pre_mix_norm, w_in, w_a2, b_a2, gla_norm, w_o_gla, q_norm, w_uq, kv_norm, w_ukv, w_o_mla, w_gate,
                 b_gate, w_out, post_mix_norm, pre_ffn_norm, w_ffn_gate, w_ffn_up, w_ffn_down, post_ffn_norm)
    for l in range(pre_mix_norm.shape[0]):
        x2 = _layer(x2, pos3, B, S, *[p[l] for p in per_layer])
    return x2.reshape(B, S, D)
```

```python
import functools

import jax
import jax.numpy as jnp
from jax import lax
from jax.experimental import pallas as pl
from jax.experimental.pallas import tpu as pltpu

CHUNK = 64
GLA_HEADS = 4
GLA_DK = 128
GLA_DV = 256
GLA_LOWRANK = 16
GLA_GATE_TEMP = 16.0
MLA_HEADS = 8
MLA_Q_LORA = 384
MLA_KV_LORA = 256
MLA_NOPE = 128
MLA_ROPE = 64
MLA_V = 128
MLA_QK = MLA_NOPE + MLA_ROPE
ROPE_THETA = 10000.0
NORM_EPS = 1e-6

GLA_KW = GLA_HEADS * GLA_DK
GLA_VW = GLA_HEADS * GLA_DV
MLA_W = MLA_HEADS * MLA_V
SMALL_W = 768
KPE_OFF = MLA_Q_LORA + MLA_KV_LORA
HA_OFF = KPE_OFF + MLA_ROPE
LANES = 128
K_AUG = 256
V_AUG = MLA_V + 16
MASK_BIAS = float(jnp.finfo(jnp.bfloat16).min)
LOG2E = 1.4426950408889634

VMEM_LIMIT_BYTES = 56 * 1024 * 1024

BF16 = jnp.bfloat16
F32 = jnp.float32

_NT = (((1,), (1,)), ((), ()))


def _dot(a, b):
    return jnp.dot(a, b, preferred_element_type=F32)


def _dot_nt(a, b):
    return lax.dot_general(a, b, _NT, preferred_element_type=F32)


def _rms(x, gain):
    ms = jnp.mean(x * x, axis=-1, keepdims=True)
    return x * lax.rsqrt(ms + NORM_EPS) * gain


def _sigmoid(z):
    return 1.0 / (1.0 + jnp.exp(-z))


def _const_spec(shape):
    nd = len(shape)
    return pl.BlockSpec(shape, lambda *_: (0,) * nd, pipeline_mode=pl.Buffered(1))


def _params(sem, flags=None):
    return pltpu.CompilerParams(dimension_semantics=sem, vmem_limit_bytes=VMEM_LIMIT_BYTES, flags=flags)


def _in_proj_kernel(x_ref, pos_ref, gn_ref, wq_ref, wkT_ref, wv_ref, wg_ref, ws_ref, wa2T_ref, ba2_ref,
                    invf_ref, qn_ref, wuqT_ref, kvn_ref, wuk_ref, wuvT_ref,
                    q_ref, kT_ref, v_ref, g_ref, laT_ref, qT_ref, k_ref, vT_ref, *, tm, tk, tiles_per_q):
    h = _rms(x_ref[...], gn_ref[...]).astype(BF16)
    small = _dot(h, ws_ref[...])
    q_ref[...] = (_dot(h, wq_ref[...]) * (GLA_DK ** -0.5)).astype(BF16)
    kT_ref[...] = _dot_nt(wkT_ref[...], h).astype(BF16)
    v_ref[...] = _dot(h, wv_ref[...]).astype(BF16)
    g_ref[...] = _dot(h, wg_ref[...]).astype(BF16)
    last = small[:, KPE_OFF:]
    zT = _dot_nt(wa2T_ref[...], last.astype(BF16)) + ba2_ref[...]
    log_sig = jnp.minimum(zT, 0.0) - jnp.log(1.0 + jnp.exp(-jnp.abs(zT)))
    laT_ref[...] = log_sig * (1.0 / GLA_GATE_TEMP)

    cq = _rms(small[:, :MLA_Q_LORA], qn_ref[...]).astype(BF16)
    ckv = _rms(small[:, MLA_Q_LORA:KPE_OFF], kvn_ref[...]).astype(BF16)
    qT = _dot_nt(wuqT_ref[...], cq)
    k_nope = _dot(ckv, wuk_ref[...]).astype(BF16)
    vT = _dot_nt(wuvT_ref[...], ckv).astype(BF16)

    half = MLA_ROPE // 2
    ang = invf_ref[...] * pos_ref[...].astype(F32)
    cos = jnp.cos(ang)
    sin = jnp.sin(ang)

    scale = MLA_QK ** -0.5 * LOG2E
    e_row = lax.broadcasted_iota(jnp.int32, (K_AUG - MLA_QK, tm), 0)
    q_pos = lax.broadcasted_iota(jnp.int32, (K_AUG - MLA_QK, tm), 1) + pl.program_id(1) * tm
    q_chunk = (q_pos % (tiles_per_q * tk)) // CHUNK
    bias = jnp.where(e_row > q_chunk, MASK_BIAS, 0.0).astype(BF16)
    for hd in range(MLA_HEADS):
        b0 = hd * MLA_QK
        o0 = hd * K_AUG
        qT_ref[o0:o0 + MLA_NOPE, :] = (qT[b0:b0 + MLA_NOPE] * scale).astype(BF16)
        x1 = qT[b0 + MLA_NOPE:b0 + MLA_NOPE + half]
        x2 = qT[b0 + MLA_NOPE + half:b0 + MLA_QK]
        qT_ref[o0 + MLA_NOPE:o0 + MLA_NOPE + half, :] = ((x1 * cos - x2 * sin) * scale).astype(BF16)
        qT_ref[o0 + MLA_NOPE + half:o0 + MLA_QK, :] = ((x2 * cos + x1 * sin) * scale).astype(BF16)
        qT_ref[o0 + MLA_QK:o0 + K_AUG, :] = bias

    zeros = jnp.zeros((LANES - MLA_ROPE, tm), F32)
    c_tab = jnp.concatenate([cos, cos, zeros], axis=0).T
    s_tab = jnp.concatenate([-sin, sin, zeros], axis=0).T
    lane = lax.broadcasted_iota(jnp.int32, last.shape, 1)
    swapped = jnp.where(lane < half, pltpu.roll(last, LANES - half, axis=1), pltpu.roll(last, half, axis=1))
    kpe = (last * c_tab + swapped * s_tab)[:, :MLA_ROPE].astype(BF16)

    ones = jnp.ones((V_AUG - MLA_V, tk), BF16)
    e = lax.broadcasted_iota(jnp.int32, (tk, K_AUG - MLA_QK), 1)
    chunk = lax.broadcasted_iota(jnp.int32, (tk, K_AUG - MLA_QK), 0) // CHUNK
    for s in range(tm // tk):
        rows = slice(s * tk, (s + 1) * tk)
        key_tile = pl.program_id(1) * (tm // tk) + s
        onehot = (e == chunk + (key_tile % tiles_per_q) * (tk // CHUNK)).astype(BF16)
        for hd in range(MLA_HEADS):
            k_ref[hd, s, :, :MLA_NOPE] = k_nope[rows, hd * MLA_NOPE:(hd + 1) * MLA_NOPE]
            k_ref[hd, s, :, MLA_NOPE:MLA_QK] = kpe[rows]
            k_ref[hd, s, :, MLA_QK:] = onehot
            vT_ref[hd, s, :MLA_V, :] = vT[hd * MLA_V:(hd + 1) * MLA_V, rows]
            vT_ref[hd, s, MLA_V:, :] = ones


def _in_proj(x2, pos3, gn, wq, wkT, wv, wg, ws, wa2T, ba2, invf, qn, wuqT, kvn, wuk, wuvT, B, S, tm, tk, tiles_per_q):
    T, D = x2.shape
    nt = S // tm
    sub = tm // tk
    row = lambda b, t: (b * nt + t, 0)
    col = lambda b, t: (0, b * nt + t)
    consts = (gn, wq, wkT, wv, wg, ws, wa2T, ba2, invf, qn, wuqT, kvn, wuk, wuvT)
    return pl.pallas_call(
        functools.partial(_in_proj_kernel, tm=tm, tk=tk, tiles_per_q=tiles_per_q),
        grid=(B, nt),
        in_specs=[pl.BlockSpec((tm, D), row), pl.BlockSpec((None, 1, tm), lambda b, t: (b, 0, t))]
        + [_const_spec(c.shape) for c in consts],
        out_specs=[
            pl.BlockSpec((tm, GLA_KW), row),
            pl.BlockSpec((GLA_KW, tm), col),
            pl.BlockSpec((tm, GLA_VW), row),
            pl.BlockSpec((tm, GLA_VW), row),
            pl.BlockSpec((GLA_KW, tm), col),
            pl.BlockSpec((None, MLA_HEADS * K_AUG, tm), lambda b, t: (b, 0, t)),
            pl.BlockSpec((None, MLA_HEADS, sub, tk, K_AUG), lambda b, t: (b, 0, t, 0, 0)),
            pl.BlockSpec((None, MLA_HEADS, sub, V_AUG, tk), lambda b, t: (b, 0, t, 0, 0)),
        ],
        out_shape=[
            jax.ShapeDtypeStruct((T, GLA_KW), BF16),
            jax.ShapeDtypeStruct((GLA_KW, T), BF16),
            jax.ShapeDtypeStruct((T, GLA_VW), BF16),
            jax.ShapeDtypeStruct((T, GLA_VW), BF16),
            jax.ShapeDtypeStruct((GLA_KW, T), F32),
            jax.ShapeDtypeStruct((B, MLA_HEADS * K_AUG, S), BF16),
            jax.ShapeDtypeStruct((B, MLA_HEADS, S // tk, tk, K_AUG), BF16),
            jax.ShapeDtypeStruct((B, MLA_HEADS, S // tk, V_AUG, tk), BF16),
        ],
        compiler_params=_params(("parallel", "parallel")),
        name="in_proj",
    )(x2, pos3, *consts)


PAIR = 2 * CHUNK


def _gla_kernel(q_ref, kT_ref, laT_ref, v_ref, g_ref, gn_ref, o_ref, state_ref, *, ts):
    @pl.when(pl.program_id(1) == 0)
    def _():
        state_ref[...] = jnp.zeros_like(state_ref)

    row = lax.broadcasted_iota(jnp.int32, (PAIR, PAIR), 0)
    col = lax.broadcasted_iota(jnp.int32, (PAIR, PAIR), 1)
    later = ((row // CHUNK == col // CHUNK) & (row > col)).astype(BF16)
    lane = lax.broadcasted_iota(jnp.int32, (GLA_DK, PAIR), 1)
    first = lane < CHUNK
    gn = gn_ref[...]

    units = [(p, h) for p in range(ts // PAIR) for h in range(GLA_HEADS)]
    tok = lambda p: slice(p * PAIR, (p + 1) * PAIR)
    ks = lambda h: slice(h * GLA_DK, (h + 1) * GLA_DK)
    vs = lambda h: slice(h * GLA_DV, (h + 1) * GLA_DV)

    to_end = {}
    for p, h in units:
        la = laT_ref[ks(h), tok(p)]
        la_hi = la.astype(BF16)
        la_lo = (la - la_hi.astype(F32)).astype(BF16)
        to_end[p, h] = _dot(la_hi, later) + _dot(la_lo, later)

    kv, dec = {}, {}
    for p, h in units:
        la = laT_ref[ks(h), tok(p)]
        kd = kT_ref[ks(h), tok(p)].astype(F32) * jnp.exp(to_end[p, h])
        v = v_ref[tok(p), vs(h)]
        kv[p, 0, h] = _dot(jnp.where(first, kd, 0.0).astype(BF16), v)
        kv[p, 1, h] = _dot(jnp.where(first, 0.0, kd).astype(BF16), v)
        dec[p, 0, h] = jnp.exp(jnp.sum(jnp.where(first, la, 0.0), axis=1, keepdims=True))
        dec[p, 1, h] = jnp.exp(jnp.sum(jnp.where(first, 0.0, la), axis=1, keepdims=True))

    states = [state_ref[h] for h in range(GLA_HEADS)]
    out = {}
    for p in range(ts // PAIR):
        for c in range(2):
            rows = slice(p * PAIR + c * CHUNK, p * PAIR + (c + 1) * CHUNK)
            for h in range(GLA_HEADS):
                states[h] = dec[p, c, h] * states[h] + kv[p, c, h]
                out[p, c, h] = _dot(q_ref[rows, ks(h)], states[h].astype(BF16))
    for h in range(GLA_HEADS):
        state_ref[h] = states[h]

    for p, h in units:
        o = _rms(jnp.concatenate([out[p, 0, h], out[p, 1, h]], axis=0), gn)
        g = g_ref[tok(p), vs(h)].astype(F32)
        o_ref[tok(p), vs(h)] = (o * (g * _sigmoid(g))).astype(BF16)


def _gla(q, kT, laT, v, g, gn, B, S, ts):
    T = B * S
    nt = S // ts
    row = lambda b, t: (b * nt + t, 0)
    col = lambda b, t: (0, b * nt + t)
    return pl.pallas_call(
        functools.partial(_gla_kernel, ts=ts),
        grid=(B, nt),
        in_specs=[
            pl.BlockSpec((ts, GLA_KW), row),
            pl.BlockSpec((GLA_KW, ts), col),
            pl.BlockSpec((GLA_KW, ts), col),
            pl.BlockSpec((ts, GLA_VW), row),
            pl.BlockSpec((ts, GLA_VW), row),
            _const_spec(gn.shape),
        ],
        out_specs=pl.BlockSpec((ts, GLA_VW), row),
        out_shape=jax.ShapeDtypeStruct((T, GLA_VW), BF16),
        scratch_shapes=[pltpu.VMEM((GLA_HEADS, GLA_DK, GLA_DV), F32)],
        compiler_params=_params(("parallel", "arbitrary")),
        name="gla",
    )(q, kT, laT, v, g, gn)


def _colmax(x):
    while x.shape[0] > 8 and x.shape[0] % 16 == 0:
        half = x.shape[0] // 2
        x = jnp.maximum(x[:half], x[half:])
    return jnp.max(x, axis=0, keepdims=True)


ATTN_SCRATCH_PER_HEAD = 10
QGROUP = 256


def _mla_attn_kernel(q_ref, qnext_ref, k_ref, vT_ref, o_ref, *scratch, tq, tk, heads):
    i = pl.program_id(2)
    r = tq // tk
    n = i * r
    per_head = [scratch[hd * ATTN_SCRATCH_PER_HEAD:(hd + 1) * ATTN_SCRATCH_PER_HEAD] for hd in range(heads)]

    def scores(hd, j, queries_ref, cols, biased):
        q0 = hd * K_AUG
        if biased:
            return _dot(k_ref[hd, j], queries_ref[q0:q0 + K_AUG, cols])
        return _dot(k_ref[hd, j, :, :MLA_QK], queries_ref[q0:q0 + MLA_QK, cols])

    def first_tile_scores(queries_ref):
        for hd, (acc_ref, m_ref, s0_ref, s1_ref, x0_ref, x1_ref, p0_ref, p1_ref, a0_ref, a1_ref) in enumerate(per_head):
            sT = scores(hd, 0, queries_ref, slice(None), False)
            s0_ref[...] = sT
            x0_ref[...] = _colmax(sT)

    @pl.when(i == 0)
    def _():
        first_tile_scores(q_ref)

    for acc_ref, m_ref, s0_ref, s1_ref, x0_ref, x1_ref, p0_ref, p1_ref, a0_ref, a1_ref in per_head:
        acc_ref[...] = jnp.zeros_like(acc_ref)
        m_ref[...] = jnp.full_like(m_ref, -jnp.inf)
        p1_ref[...] = jnp.zeros_like(p1_ref)
        a1_ref[...] = jnp.ones_like(a1_ref)

    def stage(j, slot, biased_next, c_prev, c_cur, c_next):
        for c in range(tq // QGROUP):
            cols = slice(c * QGROUP, (c + 1) * QGROUP)
            for hd, (acc_ref, m_ref, s0_ref, s1_ref, x0_ref, x1_ref, p0_ref, p1_ref, a0_ref, a1_ref) in enumerate(per_head):
                s_refs, x_refs, p_refs, a_refs = (s0_ref, s1_ref), (x0_ref, x1_ref), (p0_ref, p1_ref), (a0_ref, a1_ref)
                s_cur, x_cur, p_cur, a_cur = s_refs[slot], x_refs[slot], p_refs[slot], a_refs[slot]
                s_nxt, x_nxt, p_prv, a_prv = s_refs[1 - slot], x_refs[1 - slot], p_refs[1 - slot], a_refs[1 - slot]
                if c_next is not None and c * QGROUP >= c_next:
                    sT = scores(hd, j + 1, q_ref, cols, biased_next)
                    s_nxt[:, cols] = sT
                    x_nxt[:, cols] = _colmax(sT)
                if c_prev is not None and c * QGROUP >= c_prev:
                    pv = _dot(vT_ref[hd, jnp.maximum(j - 1, 0)], p_prv[:, cols])
                    acc_ref[:, cols] = a_prv[:, cols] * acc_ref[:, cols] + pv
                if c_cur is not None and c * QGROUP >= c_cur:
                    m_old = m_ref[:, cols]
                    m_new = jnp.maximum(m_old, x_cur[:, cols])
                    a_cur[:, cols] = jnp.exp2(m_old - m_new)
                    p_cur[:, cols] = jnp.exp2(s_cur[:, cols] - m_new).astype(BF16)
                    m_ref[:, cols] = m_new

    def body(jj, carry):
        for u in range(r):
            stage(r * jj + u, u % 2, False, 0, 0, 0)
        return carry

    lax.fori_loop(0, i, body, 0)
    own = slice(0, tk)
    for hd, (acc_ref, m_ref, s0_ref, s1_ref, x0_ref, x1_ref, p0_ref, p1_ref, a0_ref, a1_ref) in enumerate(per_head):
        sT = scores(hd, n, q_ref, own, True)
        s0_ref[:, own] = sT
        x0_ref[:, own] = _colmax(sT)
    for d in range(r):
        stage(n + d, d % 2, True, max(d - 1, 0) * tk, d * tk, (d + 1) * tk if d + 1 < r else None)
        if d == r - 2:
            first_tile_scores(qnext_ref)
    stage(n + r, r % 2, True, (r - 1) * tk, None, None)
    for hd, refs in enumerate(per_head):
        acc_ref = refs[0]
        o_ref[:, hd * MLA_V:(hd + 1) * MLA_V] = (acc_ref[:MLA_V, :] / acc_ref[MLA_V:MLA_V + 1, :]).T.astype(o_ref.dtype)


def _mla_attn(qa, k5, vT5, B, S, tq, heads):
    nk, tk = k5.shape[2], k5.shape[3]
    nq = S // tq
    assert (tq // tk) % 2 == 0 and tk % QGROUP == 0 and tq // CHUNK <= K_AUG - MLA_QK and MLA_HEADS % heads == 0
    per_head = [
        pltpu.VMEM((V_AUG, tq), F32), pltpu.VMEM((1, tq), F32),
        pltpu.VMEM((tk, tq), F32), pltpu.VMEM((tk, tq), F32),
        pltpu.VMEM((1, tq), F32), pltpu.VMEM((1, tq), F32),
        pltpu.VMEM((tk, tq), BF16), pltpu.VMEM((tk, tq), BF16),
        pltpu.VMEM((1, tq), F32), pltpu.VMEM((1, tq), F32),
    ]
    assert len(per_head) == ATTN_SCRATCH_PER_HEAD
    return pl.pallas_call(
        functools.partial(_mla_attn_kernel, tq=tq, tk=tk, heads=heads),
        grid=(B, MLA_HEADS // heads, nq),
        in_specs=[
            pl.BlockSpec((None, heads * K_AUG, tq), lambda b, h, i: (b, h, i)),
            pl.BlockSpec((None, heads * K_AUG, tq), lambda b, h, i: (b, h, jnp.minimum(i + 1, nq - 1))),
            pl.BlockSpec((None, heads, nk, tk, K_AUG), lambda b, h, i: (b, h, 0, 0, 0)),
            pl.BlockSpec((None, heads, nk, V_AUG, tk), lambda b, h, i: (b, h, 0, 0, 0)),
        ],
        out_specs=pl.BlockSpec((None, tq, heads * MLA_V), lambda b, h, i: (b, i, h)),
        out_shape=jax.ShapeDtypeStruct((B, S, MLA_W), BF16),
        scratch_shapes=per_head * heads,
        compiler_params=_params(("arbitrary", "arbitrary", "arbitrary")),
        name="mla_attn",
    )(qa, qa, k5, vT5)


def _mix_kernel(x_ref, a_ref, b_ref, gn_ref, wgate_ref, bgate_ref, woa_ref, wob_ref, wout_ref, pn_ref, o_ref):
    x = x_ref[...]
    d = x.shape[-1]
    h = _rms(x, gn_ref[...]).astype(BF16)
    gates = _sigmoid(_dot(h, wgate_ref[...]) + bgate_ref[...])
    y_a = _dot(a_ref[...], woa_ref[...])
    y_b = _dot(b_ref[...], wob_ref[...])
    mixed = (gates[:, :d] * y_a + gates[:, d:] * y_b).astype(BF16)
    o_ref[...] = x + _rms(_dot(mixed, wout_ref[...]), pn_ref[...])


def _mix(x2, a, b, gn, wgate, bgate, woa, wob, wout, pn, tm):
    T, D = x2.shape
    row = lambda i: (i, 0)
    return pl.pallas_call(
        _mix_kernel,
        grid=(T // tm,),
        in_specs=[
            pl.BlockSpec((tm, D), row), pl.BlockSpec((tm, a.shape[1]), row), pl.BlockSpec((tm, b.shape[1]), row),
            _const_spec(gn.shape), _const_spec(wgate.shape), _const_spec(bgate.shape),
            _const_spec(woa.shape), _const_spec(wob.shape), _const_spec(wout.shape), _const_spec(pn.shape),
        ],
        out_specs=pl.BlockSpec((tm, D), row),
        out_shape=jax.ShapeDtypeStruct((T, D), F32),
        compiler_params=_params(("parallel",)),
        name="mix",
    )(x2, a, b, gn, wgate, bgate, woa, wob, wout, pn)


def _ffn_kernel(x_ref, gn_ref, wg_ref, wu_ref, wd_ref, pn_ref, o_ref):
    x = x_ref[...]
    h = _rms(x, gn_ref[...]).astype(BF16)
    gate = _dot(h, wg_ref[...])
    up = _dot(h, wu_ref[...])
    act = (gate * _sigmoid(gate) * up).astype(BF16)
    o_ref[...] = x + _rms(_dot(act, wd_ref[...]), pn_ref[...])


def _ffn(x2, gn, wg, wu, wd, pn, tm):
    T, D = x2.shape
    row = lambda i: (i, 0)
    return pl.pallas_call(
        _ffn_kernel,
        grid=(T // tm,),
        in_specs=[
            pl.BlockSpec((tm, D), row),
            _const_spec(gn.shape), _const_spec(wg.shape), _const_spec(wu.shape),
            _const_spec(wd.shape), _const_spec(pn.shape),
        ],
        out_specs=pl.BlockSpec((tm, D), row),
        out_shape=jax.ShapeDtypeStruct((T, D), F32),
        compiler_params=_params(("parallel",)),
        name="ffn",
    )(x2, gn, wg, wu, wd, pn)


def _tile(n, pref):
    t = min(n, pref)
    assert n % t == 0, (n, t)
    return t


def _layer(x2, pos3, B, S, pre_mix_norm, w_in, w_a2, b_a2, gla_norm, w_o_gla, q_norm, w_uq, kv_norm, w_ukv,
           w_o_mla, w_gate, b_gate, w_out, post_mix_norm, pre_ffn_norm, w_ffn_gate, w_ffn_up, w_ffn_down,
           post_ffn_norm):
    T, D = x2.shape
    bf = lambda w: w.astype(BF16)
    r2 = lambda v: v.reshape(1, -1)

    o = 0
    wq = bf(w_in[:, o:o + GLA_KW]); o += GLA_KW
    wkT = bf(w_in[:, o:o + GLA_KW]).T; o += GLA_KW
    wv = bf(w_in[:, o:o + GLA_VW]); o += GLA_VW
    wg = bf(w_in[:, o:o + GLA_VW]); o += GLA_VW
    w_ha = w_in[:, o:o + GLA_LOWRANK]; o += GLA_LOWRANK
    w_mla = w_in[:, o:]
    pad = SMALL_W - HA_OFF - GLA_LOWRANK
    ws = bf(jnp.concatenate([w_mla, w_ha, jnp.zeros((D, pad), w_in.dtype)], axis=1))
    wa2T = jnp.zeros((GLA_KW, LANES), F32).at[:, MLA_ROPE:MLA_ROPE + GLA_LOWRANK].set(w_a2.T)
    wa2T = bf(wa2T)
    ba2 = b_a2.reshape(-1, 1)

    wuqT = bf(w_uq).T
    w_ukv3 = w_ukv.reshape(MLA_KV_LORA, MLA_HEADS, MLA_NOPE + MLA_V)
    wuk = bf(w_ukv3[:, :, :MLA_NOPE].reshape(MLA_KV_LORA, MLA_HEADS * MLA_NOPE))
    wuvT = bf(w_ukv3[:, :, MLA_NOPE:].reshape(MLA_KV_LORA, MLA_W)).T
    invf = (1.0 / (ROPE_THETA ** (jnp.arange(0, MLA_ROPE, 2, dtype=F32) / MLA_ROPE))).reshape(-1, 1)

    tm = _tile(T, 512)
    tk = _tile(S, 256)
    tq = _tile(S, 2048)
    q, kT, v, g, laT, qT, k5, vT5 = _in_proj(
        x2, pos3, r2(pre_mix_norm), wq, wkT, wv, wg, ws, wa2T, ba2, invf, r2(q_norm), wuqT, r2(kv_norm), wuk, wuvT,
        B, S, _tile(S, 512), tk, tq // tk)
    gla_o = _gla(q, kT, laT, v, g, r2(gla_norm), B, S, _tile(S, 256))
    mla_o = _mla_attn(qT, k5, vT5, B, S, tq, 1).reshape(T, MLA_W)

    x1 = _mix(x2, gla_o, mla_o, r2(pre_mix_norm), bf(w_gate), r2(b_gate), bf(w_o_gla), bf(w_o_mla), bf(w_out),
              r2(post_mix_norm), tm)
    return _ffn(x1, r2(pre_ffn_norm), bf(w_ffn_gate), bf(w_ffn_up), bf(w_ffn_down), r2(post_ffn_norm), tm)


def kernel(x, positions, pre_mix_norm, w_in, w_a2, b_a2, gla_norm, w_o_gla, q_norm, w_uq, kv_norm, w_ukv, w_o_mla,
           w_gate, b_gate, w_out, post_mix_norm, pre_ffn_norm, w_ffn_gate, w_ffn_up, w_ffn_down, post_ffn_norm):
    B, S, D = x.shape
    x2 = x.reshape(B * S, D)
    pos3 = positions.reshape(B, 1, S)
    per_layer = (pre_mix_norm, w_in, w_a2, b_a2, gla_norm, w_o_gla, q_norm, w_uq, kv_norm, w_ukv, w_o_mla, w_gate,
                 b_gate, w_out, post_mix_norm, pre_ffn_norm, w_ffn_gate, w_ffn_up, w_ffn_down, post_ffn_norm)
    for l in range(pre_mix_norm.shape[0]):
        x2 = _layer(x2, pos3, B, S, *[p[l] for p in per_layer])
    return x2.reshape(B, S, D)
```

```python
import functools

import jax
import jax.numpy as jnp
from jax import lax
from jax.experimental import pallas as pl
from jax.experimental.pallas import tpu as pltpu

CHUNK = 64
GLA_HEADS = 4
GLA_DK = 128
GLA_DV = 256
GLA_LOWRANK = 16
GLA_GATE_TEMP = 16.0
MLA_HEADS = 8
MLA_Q_LORA = 384
MLA_KV_LORA = 256
MLA_NOPE = 128
MLA_ROPE = 64
MLA_V = 128
MLA_QK = MLA_NOPE + MLA_ROPE
ROPE_THETA = 10000.0
NORM_EPS = 1e-6

GLA_KW = GLA_HEADS * GLA_DK
GLA_VW = GLA_HEADS * GLA_DV
MLA_W = MLA_HEADS * MLA_V
SMALL_W = 768
KPE_OFF = MLA_Q_LORA + MLA_KV_LORA
HA_OFF = KPE_OFF + MLA_ROPE
LANES = 128
K_AUG = 256
V_AUG = MLA_V + 16
MASK_BIAS = float(jnp.finfo(jnp.bfloat16).min)
LOG2E = 1.4426950408889634

VMEM_LIMIT_BYTES = 56 * 1024 * 1024

BF16 = jnp.bfloat16
F32 = jnp.float32

_NT = (((1,), (1,)), ((), ()))


def _dot(a, b):
    return jnp.dot(a, b, preferred_element_type=F32)


def _dot_nt(a, b):
    return lax.dot_general(a, b, _NT, preferred_element_type=F32)


def _rms(x, gain):
    ms = jnp.mean(x * x, axis=-1, keepdims=True)
    return x * lax.rsqrt(ms + NORM_EPS) * gain


def _sigmoid(z):
    return 1.0 / (1.0 + jnp.exp(-z))


def _const_spec(shape):
    nd = len(shape)
    return pl.BlockSpec(shape, lambda *_: (0,) * nd, pipeline_mode=pl.Buffered(1))


def _params(sem, flags=None):
    return pltpu.CompilerParams(dimension_semantics=sem, vmem_limit_bytes=VMEM_LIMIT_BYTES, flags=flags)


def _in_proj_kernel(x_ref, pos_ref, gn_ref, wq_ref, wkT_ref, wv_ref, wg_ref, ws_ref, wa2T_ref, ba2_ref,
                    invf_ref, qn_ref, wuqT_ref, kvn_ref, wuk_ref, wuvT_ref,
                    q_ref, kT_ref, v_ref, g_ref, laT_ref, qT_ref, k_ref, vT_ref, *, tm, tk, tiles_per_q):
    h = _rms(x_ref[...], gn_ref[...]).astype(BF16)
    small = _dot(h, ws_ref[...])
    q_ref[...] = (_dot(h, wq_ref[...]) * (GLA_DK ** -0.5)).astype(BF16)
    kT_ref[...] = _dot_nt(wkT_ref[...], h).astype(BF16)
    v_ref[...] = _dot(h, wv_ref[...]).astype(BF16)
    g_ref[...] = _dot(h, wg_ref[...]).astype(BF16)
    last = small[:, KPE_OFF:]
    zT = _dot_nt(wa2T_ref[...], last.astype(BF16)) + ba2_ref[...]
    log_sig = jnp.minimum(zT, 0.0) - jnp.log(1.0 + jnp.exp(-jnp.abs(zT)))
    laT_ref[...] = log_sig * (1.0 / GLA_GATE_TEMP)

    cq = _rms(small[:, :MLA_Q_LORA], qn_ref[...]).astype(BF16)
    ckv = _rms(small[:, MLA_Q_LORA:KPE_OFF], kvn_ref[...]).astype(BF16)
    qT = _dot_nt(wuqT_ref[...], cq)
    k_nope = _dot(ckv, wuk_ref[...]).astype(BF16)
    vT = _dot_nt(wuvT_ref[...], ckv).astype(BF16)

    half = MLA_ROPE // 2
    ang = invf_ref[...] * pos_ref[...].astype(F32)
    cos = jnp.cos(ang)
    sin = jnp.sin(ang)

    scale = MLA_QK ** -0.5 * LOG2E
    e_row = lax.broadcasted_iota(jnp.int32, (K_AUG - MLA_QK, tm), 0)
    q_pos = lax.broadcasted_iota(jnp.int32, (K_AUG - MLA_QK, tm), 1) + pl.program_id(1) * tm
    q_chunk = (q_pos % (tiles_per_q * tk)) // CHUNK
    bias = jnp.where(e_row > q_chunk, MASK_BIAS, 0.0).astype(BF16)
    for hd in range(MLA_HEADS):
        b0 = hd * MLA_QK
        o0 = hd * K_AUG
        qT_ref[o0:o0 + MLA_NOPE, :] = (qT[b0:b0 + MLA_NOPE] * scale).astype(BF16)
        x1 = qT[b0 + MLA_NOPE:b0 + MLA_NOPE + half]
        x2 = qT[b0 + MLA_NOPE + half:b0 + MLA_QK]
        qT_ref[o0 + MLA_NOPE:o0 + MLA_NOPE + half, :] = ((x1 * cos - x2 * sin) * scale).astype(BF16)
        qT_ref[o0 + MLA_NOPE + half:o0 + MLA_QK, :] = ((x2 * cos + x1 * sin) * scale).astype(BF16)
        qT_ref[o0 + MLA_QK:o0 + K_AUG, :] = bias

    zeros = jnp.zeros((LANES - MLA_ROPE, tm), F32)
    c_tab = jnp.concatenate([cos, cos, zeros], axis=0).T
    s_tab = jnp.concatenate([-sin, sin, zeros], axis=0).T
    lane = lax.broadcasted_iota(jnp.int32, last.shape, 1)
    swapped = jnp.where(lane < half, pltpu.roll(last, LANES - half, axis=1), pltpu.roll(last, half, axis=1))
    kpe = (last * c_tab + swapped * s_tab)[:, :MLA_ROPE].astype(BF16)

    ones = jnp.ones((V_AUG - MLA_V, tk), BF16)
    e = lax.broadcasted_iota(jnp.int32, (tk, K_AUG - MLA_QK), 1)
    chunk = lax.broadcasted_iota(jnp.int32, (tk, K_AUG - MLA_QK), 0) // CHUNK
    for s in range(tm // tk):
        rows = slice(s * tk, (s + 1) * tk)
        key_tile = pl.program_id(1) * (tm // tk) + s
        onehot = (e == chunk + (key_tile % tiles_per_q) * (tk // CHUNK)).astype(BF16)
        for hd in range(MLA_HEADS):
            k_ref[hd, s, :, :MLA_NOPE] = k_nope[rows, hd * MLA_NOPE:(hd + 1) * MLA_NOPE]
            k_ref[hd, s, :, MLA_NOPE:MLA_QK] = kpe[rows]
            k_ref[hd, s, :, MLA_QK:] = onehot
            vT_ref[hd, s, :MLA_V, :] = vT[hd * MLA_V:(hd + 1) * MLA_V, rows]
            vT_ref[hd, s, MLA_V:, :] = ones


def _in_proj(x2, pos3, gn, wq, wkT, wv, wg, ws, wa2T, ba2, invf, qn, wuqT, kvn, wuk, wuvT, B, S, tm, tk, tiles_per_q):
    T, D = x2.shape
    nt = S // tm
    sub = tm // tk
    row = lambda b, t: (b * nt + t, 0)
    col = lambda b, t: (0, b * nt + t)
    consts = (gn, wq, wkT, wv, wg, ws, wa2T, ba2, invf, qn, wuqT, kvn, wuk, wuvT)
    return pl.pallas_call(
        functools.partial(_in_proj_kernel, tm=tm, tk=tk, tiles_per_q=tiles_per_q),
        grid=(B, nt),
        in_specs=[pl.BlockSpec((tm, D), row), pl.BlockSpec((None, 1, tm), lambda b, t: (b, 0, t))]
        + [_const_spec(c.shape) for c in consts],
        out_specs=[
            pl.BlockSpec((tm, GLA_KW), row),
            pl.BlockSpec((GLA_KW, tm), col),
            pl.BlockSpec((tm, GLA_VW), row),
            pl.BlockSpec((tm, GLA_VW), row),
            pl.BlockSpec((GLA_KW, tm), col),
            pl.BlockSpec((None, MLA_HEADS * K_AUG, tm), lambda b, t: (b, 0, t)),
            pl.BlockSpec((None, MLA_HEADS, sub, tk, K_AUG), lambda b, t: (b, 0, t, 0, 0)),
            pl.BlockSpec((None, MLA_HEADS, sub, V_AUG, tk), lambda b, t: (b, 0, t, 0, 0)),
        ],
        out_shape=[
            jax.ShapeDtypeStruct((T, GLA_KW), BF16),
            jax.ShapeDtypeStruct((GLA_KW, T), BF16),
            jax.ShapeDtypeStruct((T, GLA_VW), BF16),
            jax.ShapeDtypeStruct((T, GLA_VW), BF16),
            jax.ShapeDtypeStruct((GLA_KW, T), F32),
            jax.ShapeDtypeStruct((B, MLA_HEADS * K_AUG, S), BF16),
            jax.ShapeDtypeStruct((B, MLA_HEADS, S // tk, tk, K_AUG), BF16),
            jax.ShapeDtypeStruct((B, MLA_HEADS, S // tk, V_AUG, tk), BF16),
        ],
        compiler_params=_params(("parallel", "parallel")),
        name="in_proj",
    )(x2, pos3, *consts)


PAIR = 2 * CHUNK


def _gla_kernel(q_ref, kT_ref, laT_ref, v_ref, g_ref, gn_ref, o_ref, state_ref, *, ts):
    @pl.when(pl.program_id(1) == 0)
    def _():
        state_ref[...] = jnp.zeros_like(state_ref)

    row = lax.broadcasted_iota(jnp.int32, (PAIR, PAIR), 0)
    col = lax.broadcasted_iota(jnp.int32, (PAIR, PAIR), 1)
    later = ((row // CHUNK == col // CHUNK) & (row > col)).astype(BF16)
    lane = lax.broadcasted_iota(jnp.int32, (GLA_DK, PAIR), 1)
    first = lane < CHUNK
    gn = gn_ref[...]

    units = [(p, h) for p in range(ts // PAIR) for h in range(GLA_HEADS)]
    tok = lambda p: slice(p * PAIR, (p + 1) * PAIR)
    ks = lambda h: slice(h * GLA_DK, (h + 1) * GLA_DK)
    vs = lambda h: slice(h * GLA_DV, (h + 1) * GLA_DV)

    to_end = {}
    for p, h in units:
        la = laT_ref[ks(h), tok(p)]
        la_hi = la.astype(BF16)
        la_lo = (la - la_hi.astype(F32)).astype(BF16)
        to_end[p, h] = _dot(la_hi, later) + _dot(la_lo, later)

    kv, dec = {}, {}
    for p, h in units:
        la = laT_ref[ks(h), tok(p)]
        kd = kT_ref[ks(h), tok(p)].astype(F32) * jnp.exp(to_end[p, h])
        v = v_ref[tok(p), vs(h)]
        kv[p, 0, h] = _dot(jnp.where(first, kd, 0.0).astype(BF16), v)
        kv[p, 1, h] = _dot(jnp.where(first, 0.0, kd).astype(BF16), v)
        dec[p, 0, h] = jnp.exp(jnp.sum(jnp.where(first, la, 0.0), axis=1, keepdims=True))
        dec[p, 1, h] = jnp.exp(jnp.sum(jnp.where(first, 0.0, la), axis=1, keepdims=True))

    states = [state_ref[h] for h in range(GLA_HEADS)]
    out = {}
    for p in range(ts // PAIR):
        for c in range(2):
            rows = slice(p * PAIR + c * CHUNK, p * PAIR + (c + 1) * CHUNK)
            for h in range(GLA_HEADS):
                states[h] = dec[p, c, h] * states[h] + kv[p, c, h]
                out[p, c, h] = _dot(q_ref[rows, ks(h)], states[h].astype(BF16))
    for h in range(GLA_HEADS):
        state_ref[h] = states[h]

    for p, h in units:
        o = _rms(jnp.concatenate([out[p, 0, h], out[p, 1, h]], axis=0), gn)
        g = g_ref[tok(p), vs(h)].astype(F32)
        o_ref[tok(p), vs(h)] = (o * (g * _sigmoid(g))).astype(BF16)


def _gla(q, kT, laT, v, g, gn, B, S, ts):
    T = B * S
    nt = S // ts
    row = lambda b, t: (b * nt + t, 0)
    col = lambda b, t: (0, b * nt + t)
    return pl.pallas_call(
        functools.partial(_gla_kernel, ts=ts),
        grid=(B, nt),
        in_specs=[
            pl.BlockSpec((ts, GLA_KW), row),
            pl.BlockSpec((GLA_KW, ts), col),
            pl.BlockSpec((GLA_KW, ts), col),
            pl.BlockSpec((ts, GLA_VW), row),
            pl.BlockSpec((ts, GLA_VW), row),
            _const_spec(gn.shape),
        ],
        out_specs=pl.BlockSpec((ts, GLA_VW), row),
        out_shape=jax.ShapeDtypeStruct((T, GLA_VW), BF16),
        scratch_shapes=[pltpu.VMEM((GLA_HEADS, GLA_DK, GLA_DV), F32)],
        compiler_params=_params(("parallel", "arbitrary")),
        name="gla",
    )(q, kT, laT, v, g, gn)


def _colmax(x):
    while x.shape[0] > 8 and x.shape[0] % 16 == 0:
        half = x.shape[0] // 2
        x = jnp.maximum(x[:half], x[half:])
    return jnp.max(x, axis=0, keepdims=True)


ATTN_SCRATCH_PER_HEAD = 10
QGROUP = 256


def _mla_attn_kernel(q_ref, qnext_ref, k_ref, vT_ref, o_ref, *scratch, tq, tk, heads):
    i = pl.program_id(2)
    r = tq // tk
    n = i * r
    per_head = [scratch[hd * ATTN_SCRATCH_PER_HEAD:(hd + 1) * ATTN_SCRATCH_PER_HEAD] for hd in range(heads)]

    def scores(hd, j, queries_ref, cols, biased):
        q0 = hd * K_AUG
        if biased:
            return _dot(k_ref[hd, j], queries_ref[q0:q0 + K_AUG, cols])
        return _dot(k_ref[hd, j, :, :MLA_QK], queries_ref[q0:q0 + MLA_QK, cols])

    def first_tile_scores(queries_ref):
        for hd, (acc_ref, m_ref, s0_ref, s1_ref, x0_ref, x1_ref, p0_ref, p1_ref, a0_ref, a1_ref) in enumerate(per_head):
            sT = scores(hd, 0, queries_ref, slice(None), False)
            s0_ref[...] = sT
            x0_ref[...] = _colmax(sT)

    @pl.when(i == 0)
    def _():
        first_tile_scores(q_ref)

    for acc_ref, m_ref, s0_ref, s1_ref, x0_ref, x1_ref, p0_ref, p1_ref, a0_ref, a1_ref in per_head:
        acc_ref[...] = jnp.zeros_like(acc_ref)
        m_ref[...] = jnp.full_like(m_ref, -jnp.inf)
        p1_ref[...] = jnp.zeros_like(p1_ref)
        a1_ref[...] = jnp.ones_like(a1_ref)

    def stage(j, slot, biased_next, c_prev, c_cur, c_next):
        for c in range(tq // QGROUP):
            cols = slice(c * QGROUP, (c + 1) * QGROUP)
            for hd, (acc_ref, m_ref, s0_ref, s1_ref, x0_ref, x1_ref, p0_ref, p1_ref, a0_ref, a1_ref) in enumerate(per_head):
                s_refs, x_refs, p_refs, a_refs = (s0_ref, s1_ref), (x0_ref, x1_ref), (p0_ref, p1_ref), (a0_ref, a1_ref)
                s_cur, x_cur, p_cur, a_cur = s_refs[slot], x_refs[slot], p_refs[slot], a_refs[slot]
                s_nxt, x_nxt, p_prv, a_prv = s_refs[1 - slot], x_refs[1 - slot], p_refs[1 - slot], a_refs[1 - slot]
                if c_next is not None and c * QGROUP >= c_next:
                    sT = scores(hd, j + 1, q_ref, cols, biased_next)
                    s_nxt[:, cols] = sT
                    x_nxt[:, cols] = _colmax(sT)
                if c_prev is not None and c * QGROUP >= c_prev:
                    pv = _dot(vT_ref[hd, jnp.maximum(j - 1, 0)], p_prv[:, cols])
                    acc_ref[:, cols] = a_prv[:, cols] * acc_ref[:, cols] + pv
                if c_cur is not None and c * QGROUP >= c_cur:
                    m_old = m_ref[:, cols]
                    m_new = jnp.maximum(m_old, x_cur[:, cols])
                    a_cur[:, cols] = jnp.exp2(m_old - m_new)
                    p_cur[:, cols] = jnp.exp2(s_cur[:, cols] - m_new).astype(BF16)
                    m_ref[:, cols] = m_new

    def body(jj, carry):
        for u in range(r):
            stage(r * jj + u, u % 2, False, 0, 0, 0)
        return carry

    lax.fori_loop(0, i, body, 0)
    own = slice(0, tk)
    for hd, (acc_ref, m_ref, s0_ref, s1_ref, x0_ref, x1_ref, p0_ref, p1_ref, a0_ref, a1_ref) in enumerate(per_head):
        sT = scores(hd, n, q_ref, own, True)
        s0_ref[:, own] = sT
        x0_ref[:, own] = _colmax(sT)
    for d in range(r):
        stage(n + d, d % 2, True, max(d - 1, 0) * tk, d * tk, (d + 1) * tk if d + 1 < r else None)
        if d == r - 2:
            first_tile_scores(qnext_ref)
    stage(n + r, r % 2, True, (r - 1) * tk, None, None)
    for hd, refs in enumerate(per_head):
        acc_ref = refs[0]
        o_ref[:, hd * MLA_V:(hd + 1) * MLA_V] = (acc_ref[:MLA_V, :] / acc_ref[MLA_V:MLA_V + 1, :]).T.astype(o_ref.dtype)


def _mla_attn(qa, k5, vT5, B, S, tq, heads):
    nk, tk = k5.shape[2], k5.shape[3]
    nq = S // tq
    assert (tq // tk) % 2 == 0 and tk % QGROUP == 0 and tq // CHUNK <= K_AUG - MLA_QK and MLA_HEADS % heads == 0
    per_head = [
        pltpu.VMEM((V_AUG, tq), F32), pltpu.VMEM((1, tq), F32),
        pltpu.VMEM((tk, tq), F32), pltpu.VMEM((tk, tq), F32),
        pltpu.VMEM((1, tq), F32), pltpu.VMEM((1, tq), F32),
        pltpu.VMEM((tk, tq), BF16), pltpu.VMEM((tk, tq), BF16),
        pltpu.VMEM((1, tq), F32), pltpu.VMEM((1, tq), F32),
    ]
    assert len(per_head) == ATTN_SCRATCH_PER_HEAD
    return pl.pallas_call(
        functools.partial(_mla_attn_kernel, tq=tq, tk=tk, heads=heads),
        grid=(B, MLA_HEADS // heads, nq),
        in_specs=[
            pl.BlockSpec((None, heads * K_AUG, tq), lambda b, h, i: (b, h, i)),
            pl.BlockSpec((None, heads * K_AUG, tq), lambda b, h, i: (b, h, jnp.minimum(i + 1, nq - 1))),
            pl.BlockSpec((None, heads, nk, tk, K_AUG), lambda b, h, i: (b, h, 0, 0, 0)),
            pl.BlockSpec((None, heads, nk, V_AUG, tk), lambda b, h, i: (b, h, 0, 0, 0)),
        ],
        out_specs=pl.BlockSpec((None, tq, heads * MLA_V), lambda b, h, i: (b, i, h)),
        out_shape=jax.ShapeDtypeStruct((B, S, MLA_W), BF16),
        scratch_shapes=per_head * heads,
        compiler_params=_params(("arbitrary", "arbitrary", "arbitrary")),
        name="mla_attn",
    )(qa, qa, k5, vT5)


def _mix_kernel(x_ref, a_ref, b_ref, gn_ref, wgate_ref, bgate_ref, woa_ref, wob_ref, wout_ref, pn_ref, o_ref):
    x = x_ref[...]
    d = x.shape[-1]
    h = _rms(x, gn_ref[...]).astype(BF16)
    gates = _sigmoid(_dot(h, wgate_ref[...]) + bgate_ref[...])
    y_a = _dot(a_ref[...], woa_ref[...])
    y_b = _dot(b_ref[...], wob_ref[...])
    mixed = (gates[:, :d] * y_a + gates[:, d:] * y_b).astype(BF16)
    o_ref[...] = x + _rms(_dot(mixed, wout_ref[...]), pn_ref[...])


def _mix(x2, a, b, gn, wgate, bgate, woa, wob, wout, pn, tm):
    T, D = x2.shape
    row = lambda i: (i, 0)
    return pl.pallas_call(
        _mix_kernel,
        grid=(T // tm,),
        in_specs=[
            pl.BlockSpec((tm, D), row), pl.BlockSpec((tm, a.shape[1]), row), pl.BlockSpec((tm, b.shape[1]), row),
            _const_spec(gn.shape), _const_spec(wgate.shape), _const_spec(bgate.shape),
            _const_spec(woa.shape), _const_spec(wob.shape), _const_spec(wout.shape), _const_spec(pn.shape),
        ],
        out_specs=pl.BlockSpec((tm, D), row),
        out_shape=jax.ShapeDtypeStruct((T, D), F32),
        compiler_params=_params(("parallel",)),
        name="mix",
    )(x2, a, b, gn, wgate, bgate, woa, wob, wout, pn)


def _ffn_kernel(x_ref, gn_ref, wg_ref, wu_ref, wd_ref, pn_ref, o_ref):
    x = x_ref[...]
    h = _rms(x, gn_ref[...]).astype(BF16)
    gate = _dot(h, wg_ref[...])
    up = _dot(h, wu_ref[...])
    act = (gate * _sigmoid(gate) * up).astype(BF16)
    o_ref[...] = x + _rms(_dot(act, wd_ref[...]), pn_ref[...])


def _ffn(x2, gn, wg, wu, wd, pn, tm):
    T, D = x2.shape
    row = lambda i: (i, 0)
    return pl.pallas_call(
        _ffn_kernel,
        grid=(T // tm,),
        in_specs=[
            pl.BlockSpec((tm, D), row),
            _const_spec(gn.shape), _const_spec(wg.shape), _const_spec(wu.shape),
            _const_spec(wd.shape), _const_spec(pn.shape),
        ],
        out_specs=pl.BlockSpec((tm, D), row),
        out_shape=jax.ShapeDtypeStruct((T, D), F32),
        compiler_params=_params(("parallel",)),
        name="ffn",
    )(x2, gn, wg, wu, wd, pn)


def _tile(n, pref):
    t = min(n, pref)
    assert n % t == 0, (n, t)
    return t


def _layer(x2, pos3, B, S, pre_mix_norm, w_in, w_a2, b_a2, gla_norm, w_o_gla, q_norm, w_uq, kv_norm, w_ukv,
           w_o_mla, w_gate, b_gate, w_out, post_mix_norm, pre_ffn_norm, w_ffn_gate, w_ffn_up, w_ffn_down,
           post_ffn_norm):
    T, D = x2.shape
    bf = lambda w: w.astype(BF16)
    r2 = lambda v: v.reshape(1, -1)

    o = 0
    wq = bf(w_in[:, o:o + GLA_KW]); o += GLA_KW
    wkT = bf(w_in[:, o:o + GLA_KW]).T; o += GLA_KW
    wv = bf(w_in[:, o:o + GLA_VW]); o += GLA_VW
    wg = bf(w_in[:, o:o + GLA_VW]); o += GLA_VW
    w_ha = w_in[:, o:o + GLA_LOWRANK]; o += GLA_LOWRANK
    w_mla = w_in[:, o:]
    pad = SMALL_W - HA_OFF - GLA_LOWRANK
    ws = bf(jnp.concatenate([w_mla, w_ha, jnp.zeros((D, pad), w_in.dtype)], axis=1))
    wa2T = jnp.zeros((GLA_KW, LANES), F32).at[:, MLA_ROPE:MLA_ROPE + GLA_LOWRANK].set(w_a2.T)
    wa2T = bf(wa2T)
    ba2 = b_a2.reshape(-1, 1)

    wuqT = bf(w_uq).T
    w_ukv3 = w_ukv.reshape(MLA_KV_LORA, MLA_HEADS, MLA_NOPE + MLA_V)
    wuk = bf(w_ukv3[:, :, :MLA_NOPE].reshape(MLA_KV_LORA, MLA_HEADS * MLA_NOPE))
    wuvT = bf(w_ukv3[:, :, MLA_NOPE:].reshape(MLA_KV_LORA, MLA_W)).T
    invf = (1.0 / (ROPE_THETA ** (jnp.arange(0, MLA_ROPE, 2, dtype=F32) / MLA_ROPE))).reshape(-1, 1)

    tm = _tile(T, 512)
    tk = _tile(S, 256)
    tq = _tile(S, 2048)
    q, kT, v, g, laT, qT, k5, vT5 = _in_proj(
        x2, pos3, r2(pre_mix_norm), wq, wkT, wv, wg, ws, wa2T, ba2, invf, r2(q_norm), wuqT, r2(kv_norm), wuk, wuvT,
        B, S, _tile(S, 512), tk, tq // tk)
    gla_o = _gla(q, kT, laT, v, g, r2(gla_norm), B, S, _tile(S, 512))
    mla_o = _mla_attn(qT, k5, vT5, B, S, tq, 1).reshape(T, MLA_W)

    x1 = _mix(x2, gla_o, mla_o, r2(pre_mix_norm), bf(w_gate), r2(b_gate), bf(w_o_gla), bf(w_o_mla), bf(w_out),
              r2(post_mix_norm), tm)
    return _ffn(x1, r2(pre_ffn_norm), bf(w_ffn_gate), bf(w_ffn_up), bf(w_ffn_down), r2(post_ffn_norm), tm)


def kernel(x, positions, pre_mix_norm, w_in, w_a2, b_a2, gla_norm, w_o_gla, q_norm, w_uq, kv_norm, w_ukv, w_o_mla,
           w_gate, b_gate, w_out, post_mix_norm, pre_ffn_norm, w_ffn_gate, w_ffn_up, w_ffn_down, post_ffn_norm):
    B, S, D = x.shape
    x2 = x.reshape(B * S, D)
    pos3 = positions.reshape(B, 1, S)
    per_layer = (pre_mix_norm, w_in, w_a2, b_a2, gla_norm, w_o_gla, q_norm, w_uq, kv_norm, w_ukv, w_o_mla, w_gate,
                 b_gate, w_out, post_mix_norm, pre_ffn_norm, w_ffn_gate, w_ffn_up, w_ffn_down, post_ffn_norm)
    for l in range(pre_mix_norm.shape[0]):
        x2 = _layer(x2, pos3, B, S, *[p[l] for p in per_layer])
    return x2.reshape(B, S, D)
```

```python
import functools

import jax
import jax.numpy as jnp
from jax import lax
from jax.experimental import pallas as pl
from jax.experimental.pallas import tpu as pltpu

CHUNK = 64
GLA_HEADS = 4
GLA_DK = 128
GLA_DV = 256
GLA_LOWRANK = 16
GLA_GATE_TEMP = 16.0
MLA_HEADS = 8
MLA_Q_LORA = 384
MLA_KV_LORA = 256
MLA_NOPE = 128
MLA_ROPE = 64
MLA_V = 128
MLA_QK = MLA_NOPE + MLA_ROPE
ROPE_THETA = 10000.0
NORM_EPS = 1e-6

GLA_KW = GLA_HEADS * GLA_DK
GLA_VW = GLA_HEADS * GLA_DV
MLA_W = MLA_HEADS * MLA_V
SMALL_W = 768
KPE_OFF = MLA_Q_LORA + MLA_KV_LORA
HA_OFF = KPE_OFF + MLA_ROPE
LANES = 128
K_AUG = 256
V_AUG = MLA_V + 16
V_TILE_ROWS = 160
MASK_BIAS = float(jnp.finfo(jnp.bfloat16).min)
LOG2E = 1.4426950408889634

VMEM_LIMIT_BYTES = 56 * 1024 * 1024

BF16 = jnp.bfloat16
F32 = jnp.float32

_NT = (((1,), (1,)), ((), ()))


def _dot(a, b):
    return jnp.dot(a, b, preferred_element_type=F32)


def _dot_nt(a, b):
    return lax.dot_general(a, b, _NT, preferred_element_type=F32)


def _rms(x, gain):
    ms = jnp.mean(x * x, axis=-1, keepdims=True)
    return x * lax.rsqrt(ms + NORM_EPS) * gain


def _sigmoid(z):
    return 1.0 / (1.0 + jnp.exp(-z))


def _const_spec(shape):
    nd = len(shape)
    return pl.BlockSpec(shape, lambda *_: (0,) * nd, pipeline_mode=pl.Buffered(1))


def _params(sem, flags=None):
    return pltpu.CompilerParams(dimension_semantics=sem, vmem_limit_bytes=VMEM_LIMIT_BYTES, flags=flags)


def _in_proj_kernel(x_ref, pos_ref, gn_ref, wq_ref, wkT_ref, wv_ref, wg_ref, ws_ref, wa2T_ref, ba2_ref,
                    invf_ref, qn_ref, wuqT_ref, kvn_ref, wuk_ref, wuvT_ref,
                    q_ref, kT_ref, v_ref, g_ref, laT_ref, qT_ref, k_ref, vT_ref, *, tm, tk, tiles_per_q):
    h = _rms(x_ref[...], gn_ref[...]).astype(BF16)
    small = _dot(h, ws_ref[...])
    q_ref[...] = (_dot(h, wq_ref[...]) * (GLA_DK ** -0.5)).astype(BF16)
    kT_ref[...] = _dot_nt(wkT_ref[...], h).astype(BF16)
    v_ref[...] = _dot(h, wv_ref[...]).astype(BF16)
    g_ref[...] = _dot(h, wg_ref[...]).astype(BF16)
    last = small[:, KPE_OFF:]
    zT = _dot_nt(wa2T_ref[...], last.astype(BF16)) + ba2_ref[...]
    log_sig = jnp.minimum(zT, 0.0) - jnp.log(1.0 + jnp.exp(-jnp.abs(zT)))
    laT_ref[...] = log_sig * (1.0 / GLA_GATE_TEMP)

    cq = _rms(small[:, :MLA_Q_LORA], qn_ref[...]).astype(BF16)
    ckv = _rms(small[:, MLA_Q_LORA:KPE_OFF], kvn_ref[...]).astype(BF16)
    qT = _dot_nt(wuqT_ref[...], cq)
    k_nope = _dot(ckv, wuk_ref[...]).astype(BF16)
    vT = _dot_nt(wuvT_ref[...], ckv).astype(BF16)

    half = MLA_ROPE // 2
    ang = invf_ref[...] * pos_ref[...].astype(F32)
    cos = jnp.cos(ang)
    sin = jnp.sin(ang)

    scale = MLA_QK ** -0.5 * LOG2E
    e_row = lax.broadcasted_iota(jnp.int32, (K_AUG - MLA_QK, tm), 0)
    q_pos = lax.broadcasted_iota(jnp.int32, (K_AUG - MLA_QK, tm), 1) + pl.program_id(1) * tm
    q_chunk = (q_pos % (tiles_per_q * tk)) // CHUNK
    bias = jnp.where(e_row > q_chunk, MASK_BIAS, 0.0).astype(BF16)
    for hd in range(MLA_HEADS):
        b0 = hd * MLA_QK
        o0 = hd * K_AUG
        qT_ref[o0:o0 + MLA_NOPE, :] = (qT[b0:b0 + MLA_NOPE] * scale).astype(BF16)
        x1 = qT[b0 + MLA_NOPE:b0 + MLA_NOPE + half]
        x2 = qT[b0 + MLA_NOPE + half:b0 + MLA_QK]
        qT_ref[o0 + MLA_NOPE:o0 + MLA_NOPE + half, :] = ((x1 * cos - x2 * sin) * scale).astype(BF16)
        qT_ref[o0 + MLA_NOPE + half:o0 + MLA_QK, :] = ((x2 * cos + x1 * sin) * scale).astype(BF16)
        qT_ref[o0 + MLA_QK:o0 + K_AUG, :] = bias

    zeros = jnp.zeros((LANES - MLA_ROPE, tm), F32)
    c_tab = jnp.concatenate([cos, cos, zeros], axis=0).T
    s_tab = jnp.concatenate([-sin, sin, zeros], axis=0).T
    lane = lax.broadcasted_iota(jnp.int32, last.shape, 1)
    swapped = jnp.where(lane < half, pltpu.roll(last, LANES - half, axis=1), pltpu.roll(last, half, axis=1))
    kpe = (last * c_tab + swapped * s_tab)[:, :MLA_ROPE].astype(BF16)

    ones = jnp.ones((V_TILE_ROWS - MLA_V, tk), BF16)
    e = lax.broadcasted_iota(jnp.int32, (tk, K_AUG - MLA_QK), 1)
    chunk = lax.broadcasted_iota(jnp.int32, (tk, K_AUG - MLA_QK), 0) // CHUNK
    for s in range(tm // tk):
        rows = slice(s * tk, (s + 1) * tk)
        key_tile = pl.program_id(1) * (tm // tk) + s
        onehot = (e == chunk + (key_tile % tiles_per_q) * (tk // CHUNK)).astype(BF16)
        for hd in range(MLA_HEADS):
            k_ref[hd, s, :, :MLA_NOPE] = k_nope[rows, hd * MLA_NOPE:(hd + 1) * MLA_NOPE]
            k_ref[hd, s, :, MLA_NOPE:MLA_QK] = kpe[rows]
            k_ref[hd, s, :, MLA_QK:] = onehot
            vT_ref[hd, s, :MLA_V, :] = vT[hd * MLA_V:(hd + 1) * MLA_V, rows]
            vT_ref[hd, s, MLA_V:, :] = ones


def _in_proj(x2, pos3, gn, wq, wkT, wv, wg, ws, wa2T, ba2, invf, qn, wuqT, kvn, wuk, wuvT, B, S, tm, tk, tiles_per_q):
    T, D = x2.shape
    nt = S // tm
    sub = tm // tk
    row = lambda b, t: (b * nt + t, 0)
    col = lambda b, t: (0, b * nt + t)
    consts = (gn, wq, wkT, wv, wg, ws, wa2T, ba2, invf, qn, wuqT, kvn, wuk, wuvT)
    return pl.pallas_call(
        functools.partial(_in_proj_kernel, tm=tm, tk=tk, tiles_per_q=tiles_per_q),
        grid=(B, nt),
        in_specs=[pl.BlockSpec((tm, D), row), pl.BlockSpec((None, 1, tm), lambda b, t: (b, 0, t))]
        + [_const_spec(c.shape) for c in consts],
        out_specs=[
            pl.BlockSpec((tm, GLA_KW), row),
            pl.BlockSpec((GLA_KW, tm), col),
            pl.BlockSpec((tm, GLA_VW), row),
            pl.BlockSpec((tm, GLA_VW), row),
            pl.BlockSpec((GLA_KW, tm), col),
            pl.BlockSpec((None, MLA_HEADS * K_AUG, tm), lambda b, t: (b, 0, t)),
            pl.BlockSpec((None, MLA_HEADS, sub, tk, K_AUG), lambda b, t: (b, 0, t, 0, 0)),
            pl.BlockSpec((None, MLA_HEADS, sub, V_TILE_ROWS, tk), lambda b, t: (b, 0, t, 0, 0)),
        ],
        out_shape=[
            jax.ShapeDtypeStruct((T, GLA_KW), BF16),
            jax.ShapeDtypeStruct((GLA_KW, T), BF16),
            jax.ShapeDtypeStruct((T, GLA_VW), BF16),
            jax.ShapeDtypeStruct((T, GLA_VW), BF16),
            jax.ShapeDtypeStruct((GLA_KW, T), F32),
            jax.ShapeDtypeStruct((B, MLA_HEADS * K_AUG, S), BF16),
            jax.ShapeDtypeStruct((B, MLA_HEADS, S // tk, tk, K_AUG), BF16),
            jax.ShapeDtypeStruct((B, MLA_HEADS, S // tk, V_TILE_ROWS, tk), BF16),
        ],
        compiler_params=_params(("parallel", "parallel")),
        name="in_proj",
    )(x2, pos3, *consts)


PAIR = 2 * CHUNK


def _gla_kernel(q_ref, kT_ref, laT_ref, v_ref, g_ref, gn_ref, o_ref, state_ref, *, ts):
    @pl.when(pl.program_id(1) == 0)
    def _():
        state_ref[...] = jnp.zeros_like(state_ref)

    row = lax.broadcasted_iota(jnp.int32, (PAIR, PAIR), 0)
    col = lax.broadcasted_iota(jnp.int32, (PAIR, PAIR), 1)
    later = ((row // CHUNK == col // CHUNK) & (row > col)).astype(BF16)
    lane = lax.broadcasted_iota(jnp.int32, (GLA_DK, PAIR), 1)
    first = lane < CHUNK
    gn = gn_ref[...]

    units = [(p, h) for p in range(ts // PAIR) for h in range(GLA_HEADS)]
    tok = lambda p: slice(p * PAIR, (p + 1) * PAIR)
    ks = lambda h: slice(h * GLA_DK, (h + 1) * GLA_DK)
    vs = lambda h: slice(h * GLA_DV, (h + 1) * GLA_DV)

    to_end = {}
    for p, h in units:
        la = laT_ref[ks(h), tok(p)]
        la_hi = la.astype(BF16)
        la_lo = (la - la_hi.astype(F32)).astype(BF16)
        to_end[p, h] = _dot(la_hi, later) + _dot(la_lo, later)

    kv, dec = {}, {}
    for p, h in units:
        la = laT_ref[ks(h), tok(p)]
        kd = kT_ref[ks(h), tok(p)].astype(F32) * jnp.exp(to_end[p, h])
        v = v_ref[tok(p), vs(h)]
        kv[p, 0, h] = _dot(jnp.where(first, kd, 0.0).astype(BF16), v)
        kv[p, 1, h] = _dot(jnp.where(first, 0.0, kd).astype(BF16), v)
        dec[p, 0, h] = jnp.exp(jnp.sum(jnp.where(first, la, 0.0), axis=1, keepdims=True))
        dec[p, 1, h] = jnp.exp(jnp.sum(jnp.where(first, 0.0, la), axis=1, keepdims=True))

    states = [state_ref[h] for h in range(GLA_HEADS)]
    out = {}
    for p in range(ts // PAIR):
        for c in range(2):
            rows = slice(p * PAIR + c * CHUNK, p * PAIR + (c + 1) * CHUNK)
            for h in range(GLA_HEADS):
                states[h] = dec[p, c, h] * states[h] + kv[p, c, h]
                out[p, c, h] = _dot(q_ref[rows, ks(h)], states[h].astype(BF16))
    for h in range(GLA_HEADS):
        state_ref[h] = states[h]

    for p, h in units:
        o = _rms(jnp.concatenate([out[p, 0, h], out[p, 1, h]], axis=0), gn)
        g = g_ref[tok(p), vs(h)].astype(F32)
        o_ref[tok(p), vs(h)] = (o * (g * _sigmoid(g))).astype(BF16)


def _gla(q, kT, laT, v, g, gn, B, S, ts):
    T = B * S
    nt = S // ts
    row = lambda b, t: (b * nt + t, 0)
    col = lambda b, t: (0, b * nt + t)
    return pl.pallas_call(
        functools.partial(_gla_kernel, ts=ts),
        grid=(B, nt),
        in_specs=[
            pl.BlockSpec((ts, GLA_KW), row),
            pl.BlockSpec((GLA_KW, ts), col),
            pl.BlockSpec((GLA_KW, ts), col),
            pl.BlockSpec((ts, GLA_VW), row),
            pl.BlockSpec((ts, GLA_VW), row),
            _const_spec(gn.shape),
        ],
        out_specs=pl.BlockSpec((ts, GLA_VW), row),
        out_shape=jax.ShapeDtypeStruct((T, GLA_VW), BF16),
        scratch_shapes=[pltpu.VMEM((GLA_HEADS, GLA_DK, GLA_DV), F32)],
        compiler_params=_params(("parallel", "arbitrary")),
        name="gla",
    )(q, kT, laT, v, g, gn)


def _colmax(x):
    while x.shape[0] > 8 and x.shape[0] % 16 == 0:
        half = x.shape[0] // 2
        x = jnp.maximum(x[:half], x[half:])
    return jnp.max(x, axis=0, keepdims=True)


ATTN_SCRATCH_PER_HEAD = 10
QGROUP = 256


def _mla_attn_kernel(q_ref, qnext_ref, k_ref, vT_ref, o_ref, *scratch, tq, tk, heads):
    i = pl.program_id(2)
    r = tq // tk
    n = i * r
    per_head = [scratch[hd * ATTN_SCRATCH_PER_HEAD:(hd + 1) * ATTN_SCRATCH_PER_HEAD] for hd in range(heads)]

    def scores(hd, j, queries_ref, cols, biased):
        q0 = hd * K_AUG
        if biased:
            return _dot(k_ref[hd, j], queries_ref[q0:q0 + K_AUG, cols])
        return _dot(k_ref[hd, j, :, :MLA_QK], queries_ref[q0:q0 + MLA_QK, cols])

    def first_tile_scores(queries_ref):
        for hd, (acc_ref, m_ref, s0_ref, s1_ref, x0_ref, x1_ref, p0_ref, p1_ref, a0_ref, a1_ref) in enumerate(per_head):
            sT = scores(hd, 0, queries_ref, slice(None), False)
            s0_ref[...] = sT
            x0_ref[...] = _colmax(sT)

    @pl.when(i == 0)
    def _():
        first_tile_scores(q_ref)

    for acc_ref, m_ref, s0_ref, s1_ref, x0_ref, x1_ref, p0_ref, p1_ref, a0_ref, a1_ref in per_head:
        acc_ref[...] = jnp.zeros_like(acc_ref)
        m_ref[...] = jnp.full_like(m_ref, -jnp.inf)
        p1_ref[...] = jnp.zeros_like(p1_ref)
        a1_ref[...] = jnp.ones_like(a1_ref)

    def stage(j, slot, biased_next, c_prev, c_cur, c_next):
        for c in range(tq // QGROUP):
            cols = slice(c * QGROUP, (c + 1) * QGROUP)
            for hd, (acc_ref, m_ref, s0_ref, s1_ref, x0_ref, x1_ref, p0_ref, p1_ref, a0_ref, a1_ref) in enumerate(per_head):
                s_refs, x_refs, p_refs, a_refs = (s0_ref, s1_ref), (x0_ref, x1_ref), (p0_ref, p1_ref), (a0_ref, a1_ref)
                s_cur, x_cur, p_cur, a_cur = s_refs[slot], x_refs[slot], p_refs[slot], a_refs[slot]
                s_nxt, x_nxt, p_prv, a_prv = s_refs[1 - slot], x_refs[1 - slot], p_refs[1 - slot], a_refs[1 - slot]
                if c_next is not None and c * QGROUP >= c_next:
                    sT = scores(hd, j + 1, q_ref, cols, biased_next)
                    s_nxt[:, cols] = sT
                    x_nxt[:, cols] = _colmax(sT)
                if c_prev is not None and c * QGROUP >= c_prev:
                    pv = _dot(vT_ref[hd, jnp.maximum(j - 1, 0), :V_AUG, :], p_prv[:, cols])
                    acc_ref[:, cols] = a_prv[:, cols] * acc_ref[:, cols] + pv
                if c_cur is not None and c * QGROUP >= c_cur:
                    m_old = m_ref[:, cols]
                    m_new = jnp.maximum(m_old, x_cur[:, cols])
                    a_cur[:, cols] = jnp.exp2(m_old - m_new)
                    p_cur[:, cols] = jnp.exp2(s_cur[:, cols] - m_new).astype(BF16)
                    m_ref[:, cols] = m_new

    def body(jj, carry):
        for u in range(r):
            stage(r * jj + u, u % 2, False, 0, 0, 0)
        return carry

    lax.fori_loop(0, i, body, 0)
    own = slice(0, tk)
    for hd, (acc_ref, m_ref, s0_ref, s1_ref, x0_ref, x1_ref, p0_ref, p1_ref, a0_ref, a1_ref) in enumerate(per_head):
        sT = scores(hd, n, q_ref, own, True)
        s0_ref[:, own] = sT
        x0_ref[:, own] = _colmax(sT)
    for d in range(r):
        stage(n + d, d % 2, True, max(d - 1, 0) * tk, d * tk, (d + 1) * tk if d + 1 < r else None)
        if d == r - 2:
            first_tile_scores(qnext_ref)
    stage(n + r, r % 2, True, (r - 1) * tk, None, None)
    for hd, refs in enumerate(per_head):
        acc_ref = refs[0]
        o_ref[:, hd * MLA_V:(hd + 1) * MLA_V] = (acc_ref[:MLA_V, :] / acc_ref[MLA_V:MLA_V + 1, :]).T.astype(o_ref.dtype)


def _mla_attn(qa, k5, vT5, B, S, tq, heads):
    nk, tk = k5.shape[2], k5.shape[3]
    nq = S // tq
    assert (tq // tk) % 2 == 0 and tk % QGROUP == 0 and tq // CHUNK <= K_AUG - MLA_QK and MLA_HEADS % heads == 0
    per_head = [
        pltpu.VMEM((V_AUG, tq), F32), pltpu.VMEM((1, tq), F32),
        pltpu.VMEM((tk, tq), F32), pltpu.VMEM((tk, tq), F32),
        pltpu.VMEM((1, tq), F32), pltpu.VMEM((1, tq), F32),
        pltpu.VMEM((tk, tq), BF16), pltpu.VMEM((tk, tq), BF16),
        pltpu.VMEM((1, tq), F32), pltpu.VMEM((1, tq), F32),
    ]
    assert len(per_head) == ATTN_SCRATCH_PER_HEAD
    return pl.pallas_call(
        functools.partial(_mla_attn_kernel, tq=tq, tk=tk, heads=heads),
        grid=(B, MLA_HEADS // heads, nq),
        in_specs=[
            pl.BlockSpec((None, heads * K_AUG, tq), lambda b, h, i: (b, h, i)),
            pl.BlockSpec((None, heads * K_AUG, tq), lambda b, h, i: (b, h, jnp.minimum(i + 1, nq - 1))),
            pl.BlockSpec((None, heads, nk, tk, K_AUG), lambda b, h, i: (b, h, 0, 0, 0)),
            pl.BlockSpec((None, heads, nk, V_TILE_ROWS, tk), lambda b, h, i: (b, h, 0, 0, 0)),
        ],
        out_specs=pl.BlockSpec((None, tq, heads * MLA_V), lambda b, h, i: (b, i, h)),
        out_shape=jax.ShapeDtypeStruct((B, S, MLA_W), BF16),
        scratch_shapes=per_head * heads,
        compiler_params=_params(("arbitrary", "arbitrary", "arbitrary")),
        name="mla_attn",
    )(qa, qa, k5, vT5)


def _mix_kernel(x_ref, a_ref, b_ref, gn_ref, wgate_ref, bgate_ref, woa_ref, wob_ref, wout_ref, pn_ref, o_ref):
    x = x_ref[...]
    d = x.shape[-1]
    h = _rms(x, gn_ref[...]).astype(BF16)
    gates = _sigmoid(_dot(h, wgate_ref[...]) + bgate_ref[...])
    y_a = _dot(a_ref[...], woa_ref[...])
    y_b = _dot(b_ref[...], wob_ref[...])
    mixed = (gates[:, :d] * y_a + gates[:, d:] * y_b).astype(BF16)
    o_ref[...] = x + _rms(_dot(mixed, wout_ref[...]), pn_ref[...])


def _mix(x2, a, b, gn, wgate, bgate, woa, wob, wout, pn, tm):
    T, D = x2.shape
    row = lambda i: (i, 0)
    return pl.pallas_call(
        _mix_kernel,
        grid=(T // tm,),
        in_specs=[
            pl.BlockSpec((tm, D), row), pl.BlockSpec((tm, a.shape[1]), row), pl.BlockSpec((tm, b.shape[1]), row),
            _const_spec(gn.shape), _const_spec(wgate.shape), _const_spec(bgate.shape),
            _const_spec(woa.shape), _const_spec(wob.shape), _const_spec(wout.shape), _const_spec(pn.shape),
        ],
        out_specs=pl.BlockSpec((tm, D), row),
        out_shape=jax.ShapeDtypeStruct((T, D), F32),
        compiler_params=_params(("parallel",)),
        name="mix",
    )(x2, a, b, gn, wgate, bgate, woa, wob, wout, pn)


def _ffn_kernel(x_ref, gn_ref, wg_ref, wu_ref, wd_ref, pn_ref, o_ref):
    x = x_ref[...]
    h = _rms(x, gn_ref[...]).astype(BF16)
    gate = _dot(h, wg_ref[...])
    up = _dot(h, wu_ref[...])
    act = (gate * _sigmoid(gate) * up).astype(BF16)
    o_ref[...] = x + _rms(_dot(act, wd_ref[...]), pn_ref[...])


def _ffn(x2, gn, wg, wu, wd, pn, tm):
    T, D = x2.shape
    row = lambda i: (i, 0)
    return pl.pallas_call(
        _ffn_kernel,
        grid=(T // tm,),
        in_specs=[
            pl.BlockSpec((tm, D), row),
            _const_spec(gn.shape), _const_spec(wg.shape), _const_spec(wu.shape),
            _const_spec(wd.shape), _const_spec(pn.shape),
        ],
        out_specs=pl.BlockSpec((tm, D), row),
        out_shape=jax.ShapeDtypeStruct((T, D), F32),
        compiler_params=_params(("parallel",)),
        name="ffn",
    )(x2, gn, wg, wu, wd, pn)


def _tile(n, pref):
    t = min(n, pref)
    assert n % t == 0, (n, t)
    return t


def _layer(x2, pos3, B, S, pre_mix_norm, w_in, w_a2, b_a2, gla_norm, w_o_gla, q_norm, w_uq, kv_norm, w_ukv,
           w_o_mla, w_gate, b_gate, w_out, post_mix_norm, pre_ffn_norm, w_ffn_gate, w_ffn_up, w_ffn_down,
           post_ffn_norm):
    T, D = x2.shape
    bf = lambda w: w.astype(BF16)
    r2 = lambda v: v.reshape(1, -1)

    o = 0
    wq = bf(w_in[:, o:o + GLA_KW]); o += GLA_KW
    wkT = bf(w_in[:, o:o + GLA_KW]).T; o += GLA_KW
    wv = bf(w_in[:, o:o + GLA_VW]); o += GLA_VW
    wg = bf(w_in[:, o:o + GLA_VW]); o += GLA_VW
    w_ha = w_in[:, o:o + GLA_LOWRANK]; o += GLA_LOWRANK
    w_mla = w_in[:, o:]
    pad = SMALL_W - HA_OFF - GLA_LOWRANK
    ws = bf(jnp.concatenate([w_mla, w_ha, jnp.zeros((D, pad), w_in.dtype)], axis=1))
    wa2T = jnp.zeros((GLA_KW, LANES), F32).at[:, MLA_ROPE:MLA_ROPE + GLA_LOWRANK].set(w_a2.T)
    wa2T = bf(wa2T)
    ba2 = b_a2.reshape(-1, 1)

    wuqT = bf(w_uq).T
    w_ukv3 = w_ukv.reshape(MLA_KV_LORA, MLA_HEADS, MLA_NOPE + MLA_V)
    wuk = bf(w_ukv3[:, :, :MLA_NOPE].reshape(MLA_KV_LORA, MLA_HEADS * MLA_NOPE))
    wuvT = bf(w_ukv3[:, :, MLA_NOPE:].reshape(MLA_KV_LORA, MLA_W)).T
    invf = (1.0 / (ROPE_THETA ** (jnp.arange(0, MLA_ROPE, 2, dtype=F32) / MLA_ROPE))).reshape(-1, 1)

    tm = _tile(T, 512)
    tk = _tile(S, 256)
    tq = _tile(S, 2048)
    q, kT, v, g, laT, qT, k5, vT5 = _in_proj(
        x2, pos3, r2(pre_mix_norm), wq, wkT, wv, wg, ws, wa2T, ba2, invf, r2(q_norm), wuqT, r2(kv_norm), wuk, wuvT,
        B, S, _tile(S, 512), tk, tq // tk)
    gla_o = _gla(q, kT, laT, v, g, r2(gla_norm), B, S, _tile(S, 512))
    mla_o = _mla_attn(qT, k5, vT5, B, S, tq, 1).reshape(T, MLA_W)

    x1 = _mix(x2, gla_o, mla_o, r2(pre_mix_norm), bf(w_gate), r2(b_gate), bf(w_o_gla), bf(w_o_mla), bf(w_out),
              r2(post_mix_norm), tm)
    return _ffn(x1, r2(pre_ffn_norm), bf(w_ffn_gate), bf(w_ffn_up), bf(w_ffn_down), r2(post_ffn_norm), tm)


def kernel(x, positions, pre_mix_norm, w_in, w_a2, b_a2, gla_norm, w_o_gla, q_norm, w_uq, kv_norm, w_ukv, w_o_mla,
           w_gate, b_gate, w_out, post_mix_norm, pre_ffn_norm, w_ffn_gate, w_ffn_up, w_ffn_down, post_ffn_norm):
    B, S, D = x.shape
    x2 = x.reshape(B * S, D)
    pos3 = positions.reshape(B, 1, S)
    per_layer = (pre_mix_norm, w_in, w_a2, b_a2, gla_norm, w_o_gla, q_norm, w_uq, kv_norm, w_ukv, w_o_mla, w_gate,
                 b_gate, w_out, post_mix_norm, pre_ffn_norm, w_ffn_gate, w_ffn_up, w_ffn_down, post_ffn_norm)
    for l in range(pre_mix_norm.shape[0]):
        x2 = _layer(x2, pos3, B, S, *[p[l] for p in per_layer])
    return x2.reshape(B, S, D)
```

```python
import functools

import jax
import jax.numpy as jnp
from jax import lax
from jax.experimental import pallas as pl
from jax.experimental.pallas import tpu as pltpu

CHUNK = 64
GLA_HEADS = 4
GLA_DK = 128
GLA_DV = 256
GLA_LOWRANK = 16
GLA_GATE_TEMP = 16.0
MLA_HEADS = 8
MLA_Q_LORA = 384
MLA_KV_LORA = 256
MLA_NOPE = 128
MLA_ROPE = 64
MLA_V = 128
MLA_QK = MLA_NOPE + MLA_ROPE
ROPE_THETA = 10000.0
NORM_EPS = 1e-6

GLA_KW = GLA_HEADS * GLA_DK
GLA_VW = GLA_HEADS * GLA_DV
MLA_W = MLA_HEADS * MLA_V
SMALL_W = 768
KPE_OFF = MLA_Q_LORA + MLA_KV_LORA
HA_OFF = KPE_OFF + MLA_ROPE
LANES = 128
K_AUG = 256
V_AUG = MLA_V + 16
MASK_BIAS = float(jnp.finfo(jnp.bfloat16).min)
LOG2E = 1.4426950408889634

VMEM_LIMIT_BYTES = 56 * 1024 * 1024

BF16 = jnp.bfloat16
F32 = jnp.float32

_NT = (((1,), (1,)), ((), ()))


def _dot(a, b):
    return jnp.dot(a, b, preferred_element_type=F32)


def _dot_nt(a, b):
    return lax.dot_general(a, b, _NT, preferred_element_type=F32)


def _rms(x, gain):
    ms = jnp.mean(x * x, axis=-1, keepdims=True)
    return x * lax.rsqrt(ms + NORM_EPS) * gain


def _sigmoid(z):
    return 1.0 / (1.0 + jnp.exp(-z))


def _const_spec(shape):
    nd = len(shape)
    return pl.BlockSpec(shape, lambda *_: (0,) * nd, pipeline_mode=pl.Buffered(1))


def _params(sem, flags=None):
    return pltpu.CompilerParams(dimension_semantics=sem, vmem_limit_bytes=VMEM_LIMIT_BYTES, flags=flags)


def _in_proj_kernel(x_ref, pos_ref, gn_ref, wq_ref, wkT_ref, wv_ref, wg_ref, ws_ref, wa2T_ref, ba2_ref,
                    invf_ref, qn_ref, wuqT_ref, kvn_ref, wuk_ref, wuvT_ref,
                    q_ref, kT_ref, v_ref, g_ref, laT_ref, qT_ref, k_ref, vT_ref, *, tm, tk, tiles_per_q):
    h = _rms(x_ref[...], gn_ref[...]).astype(BF16)
    small = _dot(h, ws_ref[...])
    q_ref[...] = (_dot(h, wq_ref[...]) * (GLA_DK ** -0.5)).astype(BF16)
    kT_ref[...] = _dot_nt(wkT_ref[...], h).astype(BF16)
    v_ref[...] = _dot(h, wv_ref[...]).astype(BF16)
    g_ref[...] = _dot(h, wg_ref[...]).astype(BF16)
    last = small[:, KPE_OFF:]
    zT = _dot_nt(wa2T_ref[...], last.astype(BF16)) + ba2_ref[...]
    log_sig = jnp.minimum(zT, 0.0) - jnp.log(1.0 + jnp.exp(-jnp.abs(zT)))
    laT_ref[...] = log_sig * (1.0 / GLA_GATE_TEMP)

    cq = _rms(small[:, :MLA_Q_LORA], qn_ref[...]).astype(BF16)
    ckv = _rms(small[:, MLA_Q_LORA:KPE_OFF], kvn_ref[...]).astype(BF16)
    qT = _dot_nt(wuqT_ref[...], cq)
    k_nope = _dot(ckv, wuk_ref[...]).astype(BF16)
    vT = _dot_nt(wuvT_ref[...], ckv).astype(BF16)

    half = MLA_ROPE // 2
    ang = invf_ref[...] * pos_ref[...].astype(F32)
    cos = jnp.cos(ang)
    sin = jnp.sin(ang)

    scale = MLA_QK ** -0.5 * LOG2E
    e_row = lax.broadcasted_iota(jnp.int32, (K_AUG - MLA_QK, tm), 0)
    q_pos = lax.broadcasted_iota(jnp.int32, (K_AUG - MLA_QK, tm), 1) + pl.program_id(1) * tm
    q_chunk = (q_pos % (tiles_per_q * tk)) // CHUNK
    bias = jnp.where(e_row > q_chunk, MASK_BIAS, 0.0).astype(BF16)
    for hd in range(MLA_HEADS):
        b0 = hd * MLA_QK
        o0 = hd * K_AUG
        qT_ref[o0:o0 + MLA_NOPE, :] = (qT[b0:b0 + MLA_NOPE] * scale).astype(BF16)
        x1 = qT[b0 + MLA_NOPE:b0 + MLA_NOPE + half]
        x2 = qT[b0 + MLA_NOPE + half:b0 + MLA_QK]
        qT_ref[o0 + MLA_NOPE:o0 + MLA_NOPE + half, :] = ((x1 * cos - x2 * sin) * scale).astype(BF16)
        qT_ref[o0 + MLA_NOPE + half:o0 + MLA_QK, :] = ((x2 * cos + x1 * sin) * scale).astype(BF16)
        qT_ref[o0 + MLA_QK:o0 + K_AUG, :] = bias

    zeros = jnp.zeros((LANES - MLA_ROPE, tm), F32)
    c_tab = jnp.concatenate([cos, cos, zeros], axis=0).T
    s_tab = jnp.concatenate([-sin, sin, zeros], axis=0).T
    lane = lax.broadcasted_iota(jnp.int32, last.shape, 1)
    swapped = jnp.where(lane < half, pltpu.roll(last, LANES - half, axis=1), pltpu.roll(last, half, axis=1))
    kpe = (last * c_tab + swapped * s_tab)[:, :MLA_ROPE].astype(BF16)

    ones = jnp.ones((V_AUG - MLA_V, tk), BF16)
    e = lax.broadcasted_iota(jnp.int32, (tk, K_AUG - MLA_QK), 1)
    chunk = lax.broadcasted_iota(jnp.int32, (tk, K_AUG - MLA_QK), 0) // CHUNK
    for s in range(tm // tk):
        rows = slice(s * tk, (s + 1) * tk)
        key_tile = pl.program_id(1) * (tm // tk) + s
        onehot = (e == chunk + (key_tile % tiles_per_q) * (tk // CHUNK)).astype(BF16)
        for hd in range(MLA_HEADS):
            k_ref[hd, s, :, :MLA_NOPE] = k_nope[rows, hd * MLA_NOPE:(hd + 1) * MLA_NOPE]
            k_ref[hd, s, :, MLA_NOPE:MLA_QK] = kpe[rows]
            k_ref[hd, s, :, MLA_QK:] = onehot
            vT_ref[hd, s, :MLA_V, :] = vT[hd * MLA_V:(hd + 1) * MLA_V, rows]
            vT_ref[hd, s, MLA_V:, :] = ones


def _in_proj(x2, pos3, gn, wq, wkT, wv, wg, ws, wa2T, ba2, invf, qn, wuqT, kvn, wuk, wuvT, B, S, tm, tk, tiles_per_q):
    T, D = x2.shape
    nt = S // tm
    sub = tm // tk
    row = lambda b, t: (b * nt + t, 0)
    col = lambda b, t: (0, b * nt + t)
    consts = (gn, wq, wkT, wv, wg, ws, wa2T, ba2, invf, qn, wuqT, kvn, wuk, wuvT)
    return pl.pallas_call(
        functools.partial(_in_proj_kernel, tm=tm, tk=tk, tiles_per_q=tiles_per_q),
        grid=(B, nt),
        in_specs=[pl.BlockSpec((tm, D), row), pl.BlockSpec((None, 1, tm), lambda b, t: (b, 0, t))]
        + [_const_spec(c.shape) for c in consts],
        out_specs=[
            pl.BlockSpec((tm, GLA_KW), row),
            pl.BlockSpec((GLA_KW, tm), col),
            pl.BlockSpec((tm, GLA_VW), row),
            pl.BlockSpec((tm, GLA_VW), row),
            pl.BlockSpec((GLA_KW, tm), col),
            pl.BlockSpec((None, MLA_HEADS * K_AUG, tm), lambda b, t: (b, 0, t)),
            pl.BlockSpec((None, MLA_HEADS, sub, tk, K_AUG), lambda b, t: (b, 0, t, 0, 0)),
            pl.BlockSpec((None, MLA_HEADS, sub, V_AUG, tk), lambda b, t: (b, 0, t, 0, 0)),
        ],
        out_shape=[
            jax.ShapeDtypeStruct((T, GLA_KW), BF16),
            jax.ShapeDtypeStruct((GLA_KW, T), BF16),
            jax.ShapeDtypeStruct((T, GLA_VW), BF16),
            jax.ShapeDtypeStruct((T, GLA_VW), BF16),
            jax.ShapeDtypeStruct((GLA_KW, T), F32),
            jax.ShapeDtypeStruct((B, MLA_HEADS * K_AUG, S), BF16),
            jax.ShapeDtypeStruct((B, MLA_HEADS, S // tk, tk, K_AUG), BF16),
            jax.ShapeDtypeStruct((B, MLA_HEADS, S // tk, V_AUG, tk), BF16),
        ],
        compiler_params=_params(("parallel", "parallel")),
        name="in_proj",
    )(x2, pos3, *consts)


PAIR = 2 * CHUNK


def _gla_kernel(q_ref, kT_ref, laT_ref, v_ref, g_ref, gn_ref, o_ref, state_ref, *, ts):
    @pl.when(pl.program_id(1) == 0)
    def _():
        state_ref[...] = jnp.zeros_like(state_ref)

    row = lax.broadcasted_iota(jnp.int32, (PAIR, PAIR), 0)
    col = lax.broadcasted_iota(jnp.int32, (PAIR, PAIR), 1)
    later = ((row // CHUNK == col // CHUNK) & (row > col)).astype(BF16)
    lane = lax.broadcasted_iota(jnp.int32, (GLA_DK, PAIR), 1)
    first = lane < CHUNK
    gn = gn_ref[...]

    units = [(p, h) for p in range(ts // PAIR) for h in range(GLA_HEADS)]
    tok = lambda p: slice(p * PAIR, (p + 1) * PAIR)
    ks = lambda h: slice(h * GLA_DK, (h + 1) * GLA_DK)
    vs = lambda h: slice(h * GLA_DV, (h + 1) * GLA_DV)

    to_end = {}
    for p, h in units:
        la = laT_ref[ks(h), tok(p)]
        la_hi = la.astype(BF16)
        la_lo = (la - la_hi.astype(F32)).astype(BF16)
        to_end[p, h] = _dot(la_hi, later) + _dot(la_lo, later)

    kv, dec = {}, {}
    for p, h in units:
        la = laT_ref[ks(h), tok(p)]
        kd = kT_ref[ks(h), tok(p)].astype(F32) * jnp.exp(to_end[p, h])
        v = v_ref[tok(p), vs(h)]
        kv[p, 0, h] = _dot(jnp.where(first, kd, 0.0).astype(BF16), v)
        kv[p, 1, h] = _dot(jnp.where(first, 0.0, kd).astype(BF16), v)
        dec[p, 0, h] = jnp.exp(jnp.sum(jnp.where(first, la, 0.0), axis=1, keepdims=True))
        dec[p, 1, h] = jnp.exp(jnp.sum(jnp.where(first, 0.0, la), axis=1, keepdims=True))

    states = [state_ref[h] for h in range(GLA_HEADS)]
    out = {}
    for p in range(ts // PAIR):
        for c in range(2):
            rows = slice(p * PAIR + c * CHUNK, p * PAIR + (c + 1) * CHUNK)
            for h in range(GLA_HEADS):
                states[h] = dec[p, c, h] * states[h] + kv[p, c, h]
                out[p, c, h] = _dot(q_ref[rows, ks(h)], states[h].astype(BF16))
    for h in range(GLA_HEADS):
        state_ref[h] = states[h]

    for p, h in units:
        o = _rms(jnp.concatenate([out[p, 0, h], out[p, 1, h]], axis=0), gn)
        g = g_ref[tok(p), vs(h)].astype(F32)
        o_ref[tok(p), vs(h)] = (o * (g * _sigmoid(g))).astype(BF16)


def _gla(q, kT, laT, v, g, gn, B, S, ts):
    T = B * S
    nt = S // ts
    row = lambda b, t: (b * nt + t, 0)
    col = lambda b, t: (0, b * nt + t)
    return pl.pallas_call(
        functools.partial(_gla_kernel, ts=ts),
        grid=(B, nt),
        in_specs=[
            pl.BlockSpec((ts, GLA_KW), row),
            pl.BlockSpec((GLA_KW, ts), col),
            pl.BlockSpec((GLA_KW, ts), col),
            pl.BlockSpec((ts, GLA_VW), row),
            pl.BlockSpec((ts, GLA_VW), row),
            _const_spec(gn.shape),
        ],
        out_specs=pl.BlockSpec((ts, GLA_VW), row),
        out_shape=jax.ShapeDtypeStruct((T, GLA_VW), BF16),
        scratch_shapes=[pltpu.VMEM((GLA_HEADS, GLA_DK, GLA_DV), F32)],
        compiler_params=_params(("parallel", "arbitrary")),
        name="gla",
    )(q, kT, laT, v, g, gn)


def _colmax(x):
    while x.shape[0] > 8 and x.shape[0] % 16 == 0:
        half = x.shape[0] // 2
        x = jnp.maximum(x[:half], x[half:])
    return jnp.max(x, axis=0, keepdims=True)


ATTN_SCRATCH_PER_HEAD = 10
QGROUP = 256


def _mla_attn_kernel(q_ref, qnext_ref, k_ref, vT_ref, o_ref, *scratch, tq, tk, heads):
    i = pl.program_id(2)
    r = tq // tk
    n = i * r
    per_head = [scratch[hd * ATTN_SCRATCH_PER_HEAD:(hd + 1) * ATTN_SCRATCH_PER_HEAD] for hd in range(heads)]

    def scores(hd, j, queries_ref, cols, biased):
        q0 = hd * K_AUG
        if biased:
            return _dot(k_ref[hd, j], queries_ref[q0:q0 + K_AUG, cols])
        return _dot(k_ref[hd, j, :, :MLA_QK], queries_ref[q0:q0 + MLA_QK, cols])

    def first_tile_scores(queries_ref):
        for hd, (acc_ref, m_ref, s0_ref, s1_ref, x0_ref, x1_ref, p0_ref, p1_ref, a0_ref, a1_ref) in enumerate(per_head):
            sT = scores(hd, 0, queries_ref, slice(None), False)
            s0_ref[...] = sT
            x0_ref[...] = _colmax(sT)

    @pl.when(i == 0)
    def _():
        first_tile_scores(q_ref)

    for acc_ref, m_ref, s0_ref, s1_ref, x0_ref, x1_ref, p0_ref, p1_ref, a0_ref, a1_ref in per_head:
        acc_ref[...] = jnp.zeros_like(acc_ref)
        m_ref[...] = jnp.full_like(m_ref, -jnp.inf)
        p1_ref[...] = jnp.zeros_like(p1_ref)
        a1_ref[...] = jnp.ones_like(a1_ref)

    def stage(j, slot, biased_next, c_prev, c_cur, c_next):
        for c in range(tq // QGROUP):
            cols = slice(c * QGROUP, (c + 1) * QGROUP)
            for hd, (acc_ref, m_ref, s0_ref, s1_ref, x0_ref, x1_ref, p0_ref, p1_ref, a0_ref, a1_ref) in enumerate(per_head):
                s_refs, x_refs, p_refs, a_refs = (s0_ref, s1_ref), (x0_ref, x1_ref), (p0_ref, p1_ref), (a0_ref, a1_ref)
                s_cur, x_cur, p_cur, a_cur = s_refs[slot], x_refs[slot], p_refs[slot], a_refs[slot]
                s_nxt, x_nxt, p_prv, a_prv = s_refs[1 - slot], x_refs[1 - slot], p_refs[1 - slot], a_refs[1 - slot]
                if c_next is not None and c * QGROUP >= c_next:
                    sT = scores(hd, j + 1, q_ref, cols, biased_next)
                    s_nxt[:, cols] = sT
                    x_nxt[:, cols] = _colmax(sT)
                if c_prev is not None and c * QGROUP >= c_prev:
                    pv = _dot(vT_ref[hd, jnp.maximum(j - 1, 0)], p_prv[:, cols])
                    acc_ref[:, cols] = a_prv[:, cols] * acc_ref[:, cols] + pv
                if c_cur is not None and c * QGROUP >= c_cur:
                    m_old = m_ref[:, cols]
                    m_new = jnp.maximum(m_old, x_cur[:, cols])
                    a_cur[:, cols] = jnp.exp2(m_old - m_new)
                    p_cur[:, cols] = jnp.exp2(s_cur[:, cols] - m_new).astype(BF16)
                    m_ref[:, cols] = m_new

    def body(jj, carry):
        for u in range(r):
            stage(r * jj + u, u % 2, False, 0, 0, 0)
        return carry

    lax.fori_loop(0, i, body, 0)
    own = slice(0, tk)
    for hd, (acc_ref, m_ref, s0_ref, s1_ref, x0_ref, x1_ref, p0_ref, p1_ref, a0_ref, a1_ref) in enumerate(per_head):
        sT = scores(hd, n, q_ref, own, True)
        s0_ref[:, own] = sT
        x0_ref[:, own] = _colmax(sT)
    for d in range(r):
        stage(n + d, d % 2, True, max(d - 1, 0) * tk, d * tk, (d + 1) * tk if d + 1 < r else None)
        if d == r - 2:
            first_tile_scores(qnext_ref)
    stage(n + r, r % 2, True, (r - 1) * tk, None, None)
    for hd, refs in enumerate(per_head):
        acc_ref = refs[0]
        o_ref[:, hd * MLA_V:(hd + 1) * MLA_V] = (acc_ref[:MLA_V, :] / acc_ref[MLA_V:MLA_V + 1, :]).T.astype(o_ref.dtype)


def _mla_attn(qa, k5, vT5, B, S, tq, heads):
    nk, tk = k5.shape[2], k5.shape[3]
    nq = S // tq
    assert (tq // tk) % 2 == 0 and tk % QGROUP == 0 and tq // CHUNK <= K_AUG - MLA_QK and MLA_HEADS % heads == 0
    per_head = [
        pltpu.VMEM((V_AUG, tq), F32), pltpu.VMEM((1, tq), F32),
        pltpu.VMEM((tk, tq), F32), pltpu.VMEM((tk, tq), F32),
        pltpu.VMEM((1, tq), F32), pltpu.VMEM((1, tq), F32),
        pltpu.VMEM((tk, tq), BF16), pltpu.VMEM((tk, tq), BF16),
        pltpu.VMEM((1, tq), F32), pltpu.VMEM((1, tq), F32),
    ]
    assert len(per_head) == ATTN_SCRATCH_PER_HEAD
    return pl.pallas_call(
        functools.partial(_mla_attn_kernel, tq=tq, tk=tk, heads=heads),
        grid=(B, MLA_HEADS // heads, nq),
        in_specs=[
            pl.BlockSpec((None, heads * K_AUG, tq), lambda b, h, i: (b, h, i)),
            pl.BlockSpec((None, heads * K_AUG, tq), lambda b, h, i: (b, h, jnp.minimum(i + 1, nq - 1))),
            pl.BlockSpec((None, heads, nk, tk, K_AUG), lambda b, h, i: (b, h, 0, 0, 0)),
            pl.BlockSpec((None, heads, nk, V_AUG, tk), lambda b, h, i: (b, h, 0, 0, 0)),
        ],
        out_specs=pl.BlockSpec((None, tq, heads * MLA_V), lambda b, h, i: (b, i, h)),
        out_shape=jax.ShapeDtypeStruct((B, S, MLA_W), BF16),
        scratch_shapes=per_head * heads,
        compiler_params=_params(("arbitrary", "arbitrary", "arbitrary")),
        name="mla_attn",
    )(qa, qa, k5, vT5)


def _mix_kernel(x_ref, a_ref, b_ref, gn_ref, wgate_ref, bgate_ref, woa_ref, wob_ref, wout_ref, pn_ref, o_ref):
    x = x_ref[...]
    d = x.shape[-1]
    h = _rms(x, gn_ref[...]).astype(BF16)
    gates = _sigmoid(_dot(h, wgate_ref[...]) + bgate_ref[...])
    y_a = _dot(a_ref[...], woa_ref[...])
    y_b = _dot(b_ref[...], wob_ref[...])
    mixed = (gates[:, :d] * y_a + gates[:, d:] * y_b).astype(BF16)
    o_ref[...] = x + _rms(_dot(mixed, wout_ref[...]), pn_ref[...])


def _mix(x2, a, b, gn, wgate, bgate, woa, wob, wout, pn, tm):
    T, D = x2.shape
    row = lambda i: (i, 0)
    return pl.pallas_call(
        _mix_kernel,
        grid=(T // tm,),
        in_specs=[
            pl.BlockSpec((tm, D), row), pl.BlockSpec((tm, a.shape[1]), row), pl.BlockSpec((tm, b.shape[1]), row),
            _const_spec(gn.shape), _const_spec(wgate.shape), _const_spec(bgate.shape),
            _const_spec(woa.shape), _const_spec(wob.shape), _const_spec(wout.shape), _const_spec(pn.shape),
        ],
        out_specs=pl.BlockSpec((tm, D), row),
        out_shape=jax.ShapeDtypeStruct((T, D), F32),
        compiler_params=_params(("parallel",)),
        name="mix",
    )(x2, a, b, gn, wgate, bgate, woa, wob, wout, pn)


def _ffn_kernel(x_ref, gn_ref, wg_ref, wu_ref, wd_ref, pn_ref, o_ref):
    x = x_ref[...]
    h = _rms(x, gn_ref[...]).astype(BF16)
    gate = _dot(h, wg_ref[...])
    up = _dot(h, wu_ref[...])
    act = (gate * _sigmoid(gate) * up).astype(BF16)
    o_ref[...] = x + _rms(_dot(act, wd_ref[...]), pn_ref[...])


def _ffn(x2, gn, wg, wu, wd, pn, tm):
    T, D = x2.shape
    row = lambda i: (i, 0)
    return pl.pallas_call(
        _ffn_kernel,
        grid=(T // tm,),
        in_specs=[
            pl.BlockSpec((tm, D), row),
            _const_spec(gn.shape), _const_spec(wg.shape), _const_spec(wu.shape),
            _const_spec(wd.shape), _const_spec(pn.shape),
        ],
        out_specs=pl.BlockSpec((tm, D), row),
        out_shape=jax.ShapeDtypeStruct((T, D), F32),
        compiler_params=_params(("parallel",)),
        name="ffn",
    )(x2, gn, wg, wu, wd, pn)


def _tile(n, pref):
    t = min(n, pref)
    assert n % t == 0, (n, t)
    return t


def _layer(x2, pos3, B, S, pre_mix_norm, w_in, w_a2, b_a2, gla_norm, w_o_gla, q_norm, w_uq, kv_norm, w_ukv,
           w_o_mla, w_gate, b_gate, w_out, post_mix_norm, pre_ffn_norm, w_ffn_gate, w_ffn_up, w_ffn_down,
           post_ffn_norm):
    T, D = x2.shape
    bf = lambda w: w.astype(BF16)
    r2 = lambda v: v.reshape(1, -1)

    o = 0
    wq = bf(w_in[:, o:o + GLA_KW]); o += GLA_KW
    wkT = bf(w_in[:, o:o + GLA_KW]).T; o += GLA_KW
    wv = bf(w_in[:, o:o + GLA_VW]); o += GLA_VW
    wg = bf(w_in[:, o:o + GLA_VW]); o += GLA_VW
    w_ha = w_in[:, o:o + GLA_LOWRANK]; o += GLA_LOWRANK
    w_mla = w_in[:, o:]
    pad = SMALL_W - HA_OFF - GLA_LOWRANK
    ws = bf(jnp.concatenate([w_mla, w_ha, jnp.zeros((D, pad), w_in.dtype)], axis=1))
    wa2T = jnp.zeros((GLA_KW, LANES), F32).at[:, MLA_ROPE:MLA_ROPE + GLA_LOWRANK].set(w_a2.T)
    wa2T = bf(wa2T)
    ba2 = b_a2.reshape(-1, 1)

    wuqT = bf(w_uq).T
    w_ukv3 = w_ukv.reshape(MLA_KV_LORA, MLA_HEADS, MLA_NOPE + MLA_V)
    wuk = bf(w_ukv3[:, :, :MLA_NOPE].reshape(MLA_KV_LORA, MLA_HEADS * MLA_NOPE))
    wuvT = bf(w_ukv3[:, :, MLA_NOPE:].reshape(MLA_KV_LORA, MLA_W)).T
    invf = (1.0 / (ROPE_THETA ** (jnp.arange(0, MLA_ROPE, 2, dtype=F32) / MLA_ROPE))).reshape(-1, 1)

    tm = _tile(T, 512)
    tk = _tile(S, 512)
    tq = _tile(S, 2048)
    q, kT, v, g, laT, qT, k5, vT5 = _in_proj(
        x2, pos3, r2(pre_mix_norm), wq, wkT, wv, wg, ws, wa2T, ba2, invf, r2(q_norm), wuqT, r2(kv_norm), wuk, wuvT,
        B, S, _tile(S, 512), tk, tq // tk)
    gla_o = _gla(q, kT, laT, v, g, r2(gla_norm), B, S, _tile(S, 512))
    mla_o = _mla_attn(qT, k5, vT5, B, S, tq, 1).reshape(T, MLA_W)

    x1 = _mix(x2, gla_o, mla_o, r2(pre_mix_norm), bf(w_gate), r2(b_gate), bf(w_o_gla), bf(w_o_mla), bf(w_out),
              r2(post_mix_norm), tm)
    return _ffn(x1, r2(pre_ffn_norm), bf(w_ffn_gate), bf(w_ffn_up), bf(w_ffn_down), r2(post_ffn_norm), tm)


def kernel(x, positions, pre_mix_norm, w_in, w_a2, b_a2, gla_norm, w_o_gla, q_norm, w_uq, kv_norm, w_ukv, w_o_mla,
           w_gate, b_gate, w_out, post_mix_norm, pre_ffn_norm, w_ffn_gate, w_ffn_up, w_ffn_down, post_ffn_norm):
    B, S, D = x.shape
    x2 = x.reshape(B * S, D)
    pos3 = positions.reshape(B, 1, S)
    per_layer = (pre_mix_norm, w_in, w_a2, b_a2, gla_norm, w_o_gla, q_norm, w_uq, kv_norm, w_ukv, w_o_mla, w_gate,
                 b_gate, w_out, post_mix_norm, pre_ffn_norm, w_ffn_gate, w_ffn_up, w_ffn_down, post_ffn_norm)
    for l in range(pre_mix_norm.shape[0]):
        x2 = _layer(x2, pos3, B, S, *[p[l] for p in per_layer])
    return x2.reshape(B, S, D)
```

```python
import functools

import jax
import jax.numpy as jnp
from jax import lax
from jax.experimental import pallas as pl
from jax.experimental.pallas import tpu as pltpu

CHUNK = 64
GLA_HEADS = 4
GLA_DK = 128
GLA_DV = 256
GLA_LOWRANK = 16
GLA_GATE_TEMP = 16.0
MLA_HEADS = 8
MLA_Q_LORA = 384
MLA_KV_LORA = 256
MLA_NOPE = 128
MLA_ROPE = 64
MLA_V = 128
MLA_QK = MLA_NOPE + MLA_ROPE
ROPE_THETA = 10000.0
NORM_EPS = 1e-6

GLA_KW = GLA_HEADS * GLA_DK
GLA_VW = GLA_HEADS * GLA_DV
MLA_W = MLA_HEADS * MLA_V
SMALL_W = 768
KPE_OFF = MLA_Q_LORA + MLA_KV_LORA
HA_OFF = KPE_OFF + MLA_ROPE
LANES = 128
K_AUG = 256
V_AUG = MLA_V + 16
MASK_BIAS = float(jnp.finfo(jnp.bfloat16).min)
LOG2E = 1.4426950408889634

VMEM_LIMIT_BYTES = 56 * 1024 * 1024

BF16 = jnp.bfloat16
F32 = jnp.float32

_NT = (((1,), (1,)), ((), ()))


def _dot(a, b):
    return jnp.dot(a, b, preferred_element_type=F32)


def _dot_nt(a, b):
    return lax.dot_general(a, b, _NT, preferred_element_type=F32)


def _rms(x, gain):
    ms = jnp.mean(x * x, axis=-1, keepdims=True)
    return x * lax.rsqrt(ms + NORM_EPS) * gain


def _sigmoid(z):
    return 1.0 / (1.0 + jnp.exp(-z))


def _const_spec(shape):
    nd = len(shape)
    return pl.BlockSpec(shape, lambda *_: (0,) * nd, pipeline_mode=pl.Buffered(1))


def _params(sem, flags=None):
    return pltpu.CompilerParams(dimension_semantics=sem, vmem_limit_bytes=VMEM_LIMIT_BYTES, flags=flags)


def _in_proj_kernel(x_ref, pos_ref, gn_ref, wq_ref, wkT_ref, wv_ref, wg_ref, ws_ref, wa2T_ref, ba2_ref,
                    invf_ref, qn_ref, wuqT_ref, kvn_ref, wuk_ref, wuvT_ref,
                    q_ref, kT_ref, v_ref, g_ref, laT_ref, qT_ref, k_ref, vT_ref, *, tm, tk, tiles_per_q):
    h = _rms(x_ref[...], gn_ref[...]).astype(BF16)
    small = _dot(h, ws_ref[...])
    q_ref[...] = (_dot(h, wq_ref[...]) * (GLA_DK ** -0.5)).astype(BF16)
    kT_ref[...] = _dot_nt(wkT_ref[...], h).astype(BF16)
    v_ref[...] = _dot(h, wv_ref[...]).astype(BF16)
    g_ref[...] = _dot(h, wg_ref[...]).astype(BF16)
    last = small[:, KPE_OFF:]
    zT = _dot_nt(wa2T_ref[...], last.astype(BF16)) + ba2_ref[...]
    log_sig = jnp.minimum(zT, 0.0) - jnp.log(1.0 + jnp.exp(-jnp.abs(zT)))
    laT_ref[...] = log_sig * (1.0 / GLA_GATE_TEMP)

    cq = _rms(small[:, :MLA_Q_LORA], qn_ref[...]).astype(BF16)
    ckv = _rms(small[:, MLA_Q_LORA:KPE_OFF], kvn_ref[...]).astype(BF16)
    qT = _dot_nt(wuqT_ref[...], cq)
    k_nope = _dot(ckv, wuk_ref[...]).astype(BF16)
    vT = _dot_nt(wuvT_ref[...], ckv).astype(BF16)

    half = MLA_ROPE // 2
    ang = invf_ref[...] * pos_ref[...].astype(F32)
    cos = jnp.cos(ang)
    sin = jnp.sin(ang)

    scale = MLA_QK ** -0.5 * LOG2E
    e_row = lax.broadcasted_iota(jnp.int32, (K_AUG - MLA_QK, tm), 0)
    q_pos = lax.broadcasted_iota(jnp.int32, (K_AUG - MLA_QK, tm), 1) + pl.program_id(1) * tm
    q_chunk = (q_pos % (tiles_per_q * tk)) // CHUNK
    bias = jnp.where(e_row > q_chunk, MASK_BIAS, 0.0).astype(BF16)
    for hd in range(MLA_HEADS):
        b0 = hd * MLA_QK
        o0 = hd * K_AUG
        qT_ref[o0:o0 + MLA_NOPE, :] = (qT[b0:b0 + MLA_NOPE] * scale).astype(BF16)
        x1 = qT[b0 + MLA_NOPE:b0 + MLA_NOPE + half]
        x2 = qT[b0 + MLA_NOPE + half:b0 + MLA_QK]
        qT_ref[o0 + MLA_NOPE:o0 + MLA_NOPE + half, :] = ((x1 * cos - x2 * sin) * scale).astype(BF16)
        qT_ref[o0 + MLA_NOPE + half:o0 + MLA_QK, :] = ((x2 * cos + x1 * sin) * scale).astype(BF16)
        qT_ref[o0 + MLA_QK:o0 + K_AUG, :] = bias

    zeros = jnp.zeros((LANES - MLA_ROPE, tm), F32)
    c_tab = jnp.concatenate([cos, cos, zeros], axis=0).T
    s_tab = jnp.concatenate([-sin, sin, zeros], axis=0).T
    lane = lax.broadcasted_iota(jnp.int32, last.shape, 1)
    swapped = jnp.where(lane < half, pltpu.roll(last, LANES - half, axis=1), pltpu.roll(last, half, axis=1))
    kpe = (last * c_tab + swapped * s_tab)[:, :MLA_ROPE].astype(BF16)

    ones = jnp.ones((V_AUG - MLA_V, tk), BF16)
    e = lax.broadcasted_iota(jnp.int32, (tk, K_AUG - MLA_QK), 1)
    chunk = lax.broadcasted_iota(jnp.int32, (tk, K_AUG - MLA_QK), 0) // CHUNK
    for s in range(tm // tk):
        rows = slice(s * tk, (s + 1) * tk)
        key_tile = pl.program_id(1) * (tm // tk) + s
        onehot = (e == chunk + (key_tile % tiles_per_q) * (tk // CHUNK)).astype(BF16)
        for hd in range(MLA_HEADS):
            k_ref[hd, s, :, :MLA_NOPE] = k_nope[rows, hd * MLA_NOPE:(hd + 1) * MLA_NOPE]
            k_ref[hd, s, :, MLA_NOPE:MLA_QK] = kpe[rows]
            k_ref[hd, s, :, MLA_QK:] = onehot
            vT_ref[hd, s, :MLA_V, :] = vT[hd * MLA_V:(hd + 1) * MLA_V, rows]
            vT_ref[hd, s, MLA_V:, :] = ones


def _in_proj(x2, pos3, gn, wq, wkT, wv, wg, ws, wa2T, ba2, invf, qn, wuqT, kvn, wuk, wuvT, B, S, tm, tk, tiles_per_q):
    T, D = x2.shape
    nt = S // tm
    sub = tm // tk
    row = lambda b, t: (b * nt + t, 0)
    col = lambda b, t: (0, b * nt + t)
    consts = (gn, wq, wkT, wv, wg, ws, wa2T, ba2, invf, qn, wuqT, kvn, wuk, wuvT)
    return pl.pallas_call(
        functools.partial(_in_proj_kernel, tm=tm, tk=tk, tiles_per_q=tiles_per_q),
        grid=(B, nt),
        in_specs=[pl.BlockSpec((tm, D), row), pl.BlockSpec((None, 1, tm), lambda b, t: (b, 0, t))]
        + [_const_spec(c.shape) for c in consts],
        out_specs=[
            pl.BlockSpec((tm, GLA_KW), row),
            pl.BlockSpec((GLA_KW, tm), col),
            pl.BlockSpec((tm, GLA_VW), row),
            pl.BlockSpec((tm, GLA_VW), row),
            pl.BlockSpec((GLA_KW, tm), col),
            pl.BlockSpec((None, MLA_HEADS * K_AUG, tm), lambda b, t: (b, 0, t)),
            pl.BlockSpec((None, MLA_HEADS, sub, tk, K_AUG), lambda b, t: (b, 0, t, 0, 0)),
            pl.BlockSpec((None, MLA_HEADS, sub, V_AUG, tk), lambda b, t: (b, 0, t, 0, 0)),
        ],
        out_shape=[
            jax.ShapeDtypeStruct((T, GLA_KW), BF16),
            jax.ShapeDtypeStruct((GLA_KW, T), BF16),
            jax.ShapeDtypeStruct((T, GLA_VW), BF16),
            jax.ShapeDtypeStruct((T, GLA_VW), BF16),
            jax.ShapeDtypeStruct((GLA_KW, T), F32),
            jax.ShapeDtypeStruct((B, MLA_HEADS * K_AUG, S), BF16),
            jax.ShapeDtypeStruct((B, MLA_HEADS, S // tk, tk, K_AUG), BF16),
            jax.ShapeDtypeStruct((B, MLA_HEADS, S // tk, V_AUG, tk), BF16),
        ],
        compiler_params=_params(("parallel", "parallel")),
        name="in_proj",
    )(x2, pos3, *consts)


PAIR = 2 * CHUNK


def _gla_kernel(q_ref, kT_ref, laT_ref, v_ref, g_ref, gn_ref, o_ref, state_ref, *, ts):
    @pl.when(pl.program_id(1) == 0)
    def _():
        state_ref[...] = jnp.zeros_like(state_ref)

    row = lax.broadcasted_iota(jnp.int32, (PAIR, PAIR), 0)
    col = lax.broadcasted_iota(jnp.int32, (PAIR, PAIR), 1)
    later = ((row // CHUNK == col // CHUNK) & (row > col)).astype(BF16)
    lane = lax.broadcasted_iota(jnp.int32, (GLA_DK, PAIR), 1)
    first = lane < CHUNK
    gn = gn_ref[...]

    units = [(p, h) for p in range(ts // PAIR) for h in range(GLA_HEADS)]
    tok = lambda p: slice(p * PAIR, (p + 1) * PAIR)
    ks = lambda h: slice(h * GLA_DK, (h + 1) * GLA_DK)
    vs = lambda h: slice(h * GLA_DV, (h + 1) * GLA_DV)

    to_end = {}
    for p, h in units:
        la = laT_ref[ks(h), tok(p)]
        la_hi = la.astype(BF16)
        la_lo = (la - la_hi.astype(F32)).astype(BF16)
        to_end[p, h] = _dot(la_hi, later) + _dot(la_lo, later)

    kv, dec = {}, {}
    for p, h in units:
        la = laT_ref[ks(h), tok(p)]
        kd = kT_ref[ks(h), tok(p)].astype(F32) * jnp.exp(to_end[p, h])
        v = v_ref[tok(p), vs(h)]
        kv[p, 0, h] = _dot(jnp.where(first, kd, 0.0).astype(BF16), v)
        kv[p, 1, h] = _dot(jnp.where(first, 0.0, kd).astype(BF16), v)
        dec[p, 0, h] = jnp.exp(jnp.sum(jnp.where(first, la, 0.0), axis=1, keepdims=True))
        dec[p, 1, h] = jnp.exp(jnp.sum(jnp.where(first, 0.0, la), axis=1, keepdims=True))

    states = [state_ref[h] for h in range(GLA_HEADS)]
    out = {}
    for p in range(ts // PAIR):
        for c in range(2):
            rows = slice(p * PAIR + c * CHUNK, p * PAIR + (c + 1) * CHUNK)
            for h in range(GLA_HEADS):
                states[h] = dec[p, c, h] * states[h] + kv[p, c, h]
                out[p, c, h] = _dot(q_ref[rows, ks(h)], states[h].astype(BF16))
    for h in range(GLA_HEADS):
        state_ref[h] = states[h]

    for p, h in units:
        o = _rms(jnp.concatenate([out[p, 0, h], out[p, 1, h]], axis=0), gn)
        g = g_ref[tok(p), vs(h)].astype(F32)
        o_ref[tok(p), vs(h)] = (o * (g * _sigmoid(g))).astype(BF16)


def _gla(q, kT, laT, v, g, gn, B, S, ts):
    T = B * S
    nt = S // ts
    row = lambda b, t: (b * nt + t, 0)
    col = lambda b, t: (0, b * nt + t)
    return pl.pallas_call(
        functools.partial(_gla_kernel, ts=ts),
        grid=(B, nt),
        in_specs=[
            pl.BlockSpec((ts, GLA_KW), row),
            pl.BlockSpec((GLA_KW, ts), col),
            pl.BlockSpec((GLA_KW, ts), col),
            pl.BlockSpec((ts, GLA_VW), row),
            pl.BlockSpec((ts, GLA_VW), row),
            _const_spec(gn.shape),
        ],
        out_specs=pl.BlockSpec((ts, GLA_VW), row),
        out_shape=jax.ShapeDtypeStruct((T, GLA_VW), BF16),
        scratch_shapes=[pltpu.VMEM((GLA_HEADS, GLA_DK, GLA_DV), F32)],
        compiler_params=_params(("parallel", "arbitrary")),
        name="gla",
    )(q, kT, laT, v, g, gn)


def _colmax(x):
    while x.shape[0] > 8 and x.shape[0] % 16 == 0:
        half = x.shape[0] // 2
        x = jnp.maximum(x[:half], x[half:])
    return jnp.max(x, axis=0, keepdims=True)


ATTN_SCRATCH_PER_HEAD = 10
QGROUP = 256


def _mla_attn_kernel(q_ref, qnext_ref, k_ref, vT_ref, o_ref, *scratch, tq, tk, heads):
    i = pl.program_id(2)
    r = tq // tk
    n = i * r
    per_head = [scratch[hd * ATTN_SCRATCH_PER_HEAD:(hd + 1) * ATTN_SCRATCH_PER_HEAD] for hd in range(heads)]

    def scores(hd, j, queries_ref, cols, biased):
        q0 = hd * K_AUG
        if biased:
            return _dot(k_ref[hd, j], queries_ref[q0:q0 + K_AUG, cols])
        return _dot(k_ref[hd, j, :, :MLA_QK], queries_ref[q0:q0 + MLA_QK, cols])

    def first_tile_scores(queries_ref):
        for hd, (acc_ref, m_ref, s0_ref, s1_ref, x0_ref, x1_ref, p0_ref, p1_ref, a0_ref, a1_ref) in enumerate(per_head):
            sT = scores(hd, 0, queries_ref, slice(None), False)
            s0_ref[...] = sT
            x0_ref[...] = _colmax(sT)

    @pl.when(i == 0)
    def _():
        first_tile_scores(q_ref)

    for acc_ref, m_ref, s0_ref, s1_ref, x0_ref, x1_ref, p0_ref, p1_ref, a0_ref, a1_ref in per_head:
        acc_ref[...] = jnp.zeros_like(acc_ref)
        m_ref[...] = jnp.full_like(m_ref, -jnp.inf)
        p1_ref[...] = jnp.zeros_like(p1_ref)
        a1_ref[...] = jnp.ones_like(a1_ref)

    def stage(j, slot, biased_next, c_prev, c_cur, c_next):
        for c in range(tq // QGROUP):
            cols = slice(c * QGROUP, (c + 1) * QGROUP)
            for hd, (acc_ref, m_ref, s0_ref, s1_ref, x0_ref, x1_ref, p0_ref, p1_ref, a0_ref, a1_ref) in enumerate(per_head):
                s_refs, x_refs, p_refs, a_refs = (s0_ref, s1_ref), (x0_ref, x1_ref), (p0_ref, p1_ref), (a0_ref, a1_ref)
                s_cur, x_cur, p_cur, a_cur = s_refs[slot], x_refs[slot], p_refs[slot], a_refs[slot]
                s_nxt, x_nxt, p_prv, a_prv = s_refs[1 - slot], x_refs[1 - slot], p_refs[1 - slot], a_refs[1 - slot]
                if c_next is not None and c * QGROUP >= c_next:
                    sT = scores(hd, j + 1, q_ref, cols, biased_next)
                    s_nxt[:, cols] = sT
                    x_nxt[:, cols] = _colmax(sT)
                if c_prev is not None and c * QGROUP >= c_prev:
                    pv = _dot(vT_ref[hd, jnp.maximum(j - 1, 0)], p_prv[:, cols])
                    acc_ref[:, cols] = a_prv[:, cols] * acc_ref[:, cols] + pv
                if c_cur is not None and c * QGROUP >= c_cur:
                    m_old = m_ref[:, cols]
                    m_new = jnp.maximum(m_old, x_cur[:, cols])
                    a_cur[:, cols] = jnp.exp2(m_old - m_new)
                    p_cur[:, cols] = jnp.exp2(s_cur[:, cols] - m_new).astype(BF16)
                    m_ref[:, cols] = m_new

    def body(jj, carry):
        for u in range(r):
            stage(r * jj + u, u % 2, False, 0, 0, 0)
        return carry

    lax.fori_loop(0, i, body, 0)
    own = slice(0, tk)
    for hd, (acc_ref, m_ref, s0_ref, s1_ref, x0_ref, x1_ref, p0_ref, p1_ref, a0_ref, a1_ref) in enumerate(per_head):
        sT = scores(hd, n, q_ref, own, True)
        s0_ref[:, own] = sT
        x0_ref[:, own] = _colmax(sT)
    for d in range(r):
        stage(n + d, d % 2, True, max(d - 1, 0) * tk, d * tk, (d + 1) * tk if d + 1 < r else None)
        if d == r - 2:
            first_tile_scores(qnext_ref)
    stage(n + r, r % 2, True, (r - 1) * tk, None, None)
    for hd, refs in enumerate(per_head):
        acc_ref = refs[0]
        o_ref[:, hd * MLA_V:(hd + 1) * MLA_V] = (acc_ref[:MLA_V, :] / acc_ref[MLA_V:MLA_V + 1, :]).T.astype(o_ref.dtype)


def _mla_attn(qa, k5, vT5, B, S, tq, heads):
    nk, tk = k5.shape[2], k5.shape[3]
    nq = S // tq
    assert (tq // tk) % 2 == 0 and tk % QGROUP == 0 and tq // CHUNK <= K_AUG - MLA_QK and MLA_HEADS % heads == 0
    per_head = [
        pltpu.VMEM((V_AUG, tq), F32), pltpu.VMEM((1, tq), F32),
        pltpu.VMEM((tk, tq), F32), pltpu.VMEM((tk, tq), F32),
        pltpu.VMEM((1, tq), F32), pltpu.VMEM((1, tq), F32),
        pltpu.VMEM((tk, tq), BF16), pltpu.VMEM((tk, tq), BF16),
        pltpu.VMEM((1, tq), F32), pltpu.VMEM((1, tq), F32),
    ]
    assert len(per_head) == ATTN_SCRATCH_PER_HEAD
    return pl.pallas_call(
        functools.partial(_mla_attn_kernel, tq=tq, tk=tk, heads=heads),
        grid=(B, MLA_HEADS // heads, nq),
        in_specs=[
            pl.BlockSpec((None, heads * K_AUG, tq), lambda b, h, i: (b, h, i)),
            pl.BlockSpec((None, heads * K_AUG, tq), lambda b, h, i: (b, h, jnp.minimum(i + 1, nq - 1))),
            pl.BlockSpec((None, heads, nk, tk, K_AUG), lambda b, h, i: (b, h, 0, 0, 0)),
            pl.BlockSpec((None, heads, nk, V_AUG, tk), lambda b, h, i: (b, h, 0, 0, 0)),
        ],
        out_specs=pl.BlockSpec((None, tq, heads * MLA_V), lambda b, h, i: (b, i, h)),
        out_shape=jax.ShapeDtypeStruct((B, S, MLA_W), BF16),
        scratch_shapes=per_head * heads,
        compiler_params=_params(("arbitrary", "arbitrary", "arbitrary")),
        name="mla_attn",
    )(qa, qa, k5, vT5)


def _mix_kernel(x_ref, a_ref, b_ref, gn_ref, wgate_ref, bgate_ref, woa_ref, wob_ref, wout_ref, pn_ref, o_ref):
    x = x_ref[...]
    d = x.shape[-1]
    h = _rms(x, gn_ref[...]).astype(BF16)
    gates = _sigmoid(_dot(h, wgate_ref[...]) + bgate_ref[...])
    y_a = _dot(a_ref[...], woa_ref[...])
    y_b = _dot(b_ref[...], wob_ref[...])
    mixed = (gates[:, :d] * y_a + gates[:, d:] * y_b).astype(BF16)
    o_ref[...] = x + _rms(_dot(mixed, wout_ref[...]), pn_ref[...])


def _mix(x2, a, b, gn, wgate, bgate, woa, wob, wout, pn, tm):
    T, D = x2.shape
    row = lambda i: (i, 0)
    return pl.pallas_call(
        _mix_kernel,
        grid=(T // tm,),
        in_specs=[
            pl.BlockSpec((tm, D), row), pl.BlockSpec((tm, a.shape[1]), row), pl.BlockSpec((tm, b.shape[1]), row),
            _const_spec(gn.shape), _const_spec(wgate.shape), _const_spec(bgate.shape),
            _const_spec(woa.shape), _const_spec(wob.shape), _const_spec(wout.shape), _const_spec(pn.shape),
        ],
        out_specs=pl.BlockSpec((tm, D), row),
        out_shape=jax.ShapeDtypeStruct((T, D), F32),
        compiler_params=_params(("parallel",)),
        name="mix",
    )(x2, a, b, gn, wgate, bgate, woa, wob, wout, pn)


def _ffn_kernel(x_ref, gn_ref, wg_ref, wu_ref, wd_ref, pn_ref, o_ref):
    x = x_ref[...]
    h = _rms(x, gn_ref[...]).astype(BF16)
    gate = _dot(h, wg_ref[...])
    up = _dot(h, wu_ref[...])
    act = (gate * _sigmoid(gate) * up).astype(BF16)
    o_ref[...] = x + _rms(_dot(act, wd_ref[...]), pn_ref[...])


def _ffn(x2, gn, wg, wu, wd, pn, tm):
    T, D = x2.shape
    row = lambda i: (i, 0)
    return pl.pallas_call(
        _ffn_kernel,
        grid=(T // tm,),
        in_specs=[
            pl.BlockSpec((tm, D), row),
            _const_spec(gn.shape), _const_spec(wg.shape), _const_spec(wu.shape),
            _const_spec(wd.shape), _const_spec(pn.shape),
        ],
        out_specs=pl.BlockSpec((tm, D), row),
        out_shape=jax.ShapeDtypeStruct((T, D), F32),
        compiler_params=_params(("parallel",)),
        name="ffn",
    )(x2, gn, wg, wu, wd, pn)


def _tile(n, pref):
    t = min(n, pref)
    assert n % t == 0, (n, t)
    return t


def _layer(x2, pos3, B, S, pre_mix_norm, w_in, w_a2, b_a2, gla_norm, w_o_gla, q_norm, w_uq, kv_norm, w_ukv,
           w_o_mla, w_gate, b_gate, w_out, post_mix_norm, pre_ffn_norm, w_ffn_gate, w_ffn_up, w_ffn_down,
           post_ffn_norm):
    T, D = x2.shape
    bf = lambda w: w.astype(BF16)
    r2 = lambda v: v.reshape(1, -1)

    o = 0
    wq = bf(w_in[:, o:o + GLA_KW]); o += GLA_KW
    wkT = bf(w_in[:, o:o + GLA_KW]).T; o += GLA_KW
    wv = bf(w_in[:, o:o + GLA_VW]); o += GLA_VW
    wg = bf(w_in[:, o:o + GLA_VW]); o += GLA_VW
    w_ha = w_in[:, o:o + GLA_LOWRANK]; o += GLA_LOWRANK
    w_mla = w_in[:, o:]
    pad = SMALL_W - HA_OFF - GLA_LOWRANK
    ws = bf(jnp.concatenate([w_mla, w_ha, jnp.zeros((D, pad), w_in.dtype)], axis=1))
    wa2T = jnp.zeros((GLA_KW, LANES), F32).at[:, MLA_ROPE:MLA_ROPE + GLA_LOWRANK].set(w_a2.T)
    wa2T = bf(wa2T)
    ba2 = b_a2.reshape(-1, 1)

    wuqT = bf(w_uq).T
    w_ukv3 = w_ukv.reshape(MLA_KV_LORA, MLA_HEADS, MLA_NOPE + MLA_V)
    wuk = bf(w_ukv3[:, :, :MLA_NOPE].reshape(MLA_KV_LORA, MLA_HEADS * MLA_NOPE))
    wuvT = bf(w_ukv3[:, :, MLA_NOPE:].reshape(MLA_KV_LORA, MLA_W)).T
    invf = (1.0 / (ROPE_THETA ** (jnp.arange(0, MLA_ROPE, 2, dtype=F32) / MLA_ROPE))).reshape(-1, 1)

    tm = _tile(T, 1024)
    tk = _tile(S, 256)
    tq = _tile(S, 2048)
    q, kT, v, g, laT, qT, k5, vT5 = _in_proj(
        x2, pos3, r2(pre_mix_norm), wq, wkT, wv, wg, ws, wa2T, ba2, invf, r2(q_norm), wuqT, r2(kv_norm), wuk, wuvT,
        B, S, _tile(S, 512), tk, tq // tk)
    gla_o = _gla(q, kT, laT, v, g, r2(gla_norm), B, S, _tile(S, 1024))
    mla_o = _mla_attn(qT, k5, vT5, B, S, tq, 1).reshape(T, MLA_W)

    x1 = _mix(x2, gla_o, mla_o, r2(pre_mix_norm), bf(w_gate), r2(b_gate), bf(w_o_gla), bf(w_o_mla), bf(w_out),
              r2(post_mix_norm), tm)
    return _ffn(x1, r2(pre_ffn_norm), bf(w_ffn_gate), bf(w_ffn_up), bf(w_ffn_down), r2(post_ffn_norm), tm)


def kernel(x, positions, pre_mix_norm, w_in, w_a2, b_a2, gla_norm, w_o_gla, q_norm, w_uq, kv_norm, w_ukv, w_o_mla,
           w_gate, b_gate, w_out, post_mix_norm, pre_ffn_norm, w_ffn_gate, w_ffn_up, w_ffn_down, post_ffn_norm):
    B, S, D = x.shape
    x2 = x.reshape(B * S, D)
    pos3 = positions.reshape(B, 1, S)
    per_layer = (pre_mix_norm, w_in, w_a2, b_a2, gla_norm, w_o_gla, q_norm, w_uq, kv_norm, w_ukv, w_o_mla, w_gate,
                 b_gate, w_out, post_mix_norm, pre_ffn_norm, w_ffn_gate, w_ffn_up, w_ffn_down, post_ffn_norm)
    for l in range(pre_mix_norm.shape[0]):
        x2 = _layer(x2, pos3, B, S, *[p[l] for p in per_layer])
    return x2.reshape(B, S, D)
```

```python
import functools

import jax
import jax.numpy as jnp
from jax import lax
from jax.experimental import pallas as pl
from jax.experimental.pallas import tpu as pltpu

CHUNK = 64
GLA_HEADS = 4
GLA_DK = 128
GLA_DV = 256
GLA_LOWRANK = 16
GLA_GATE_TEMP = 16.0
MLA_HEADS = 8
MLA_Q_LORA = 384
MLA_KV_LORA = 256
MLA_NOPE = 128
MLA_ROPE = 64
MLA_V = 128
MLA_QK = MLA_NOPE + MLA_ROPE
ROPE_THETA = 10000.0
NORM_EPS = 1e-6

GLA_KW = GLA_HEADS * GLA_DK
GLA_VW = GLA_HEADS * GLA_DV
MLA_W = MLA_HEADS * MLA_V
SMALL_W = 768
KPE_OFF = MLA_Q_LORA + MLA_KV_LORA
HA_OFF = KPE_OFF + MLA_ROPE
LANES = 128
K_AUG = 256
V_AUG = MLA_V + 16
MASK_BIAS = float(jnp.finfo(jnp.bfloat16).min)
LOG2E = 1.4426950408889634

VMEM_LIMIT_BYTES = 56 * 1024 * 1024

BF16 = jnp.bfloat16
F32 = jnp.float32

_NT = (((1,), (1,)), ((), ()))


def _dot(a, b):
    return jnp.dot(a, b, preferred_element_type=F32)


def _dot_nt(a, b):
    return lax.dot_general(a, b, _NT, preferred_element_type=F32)


def _rms(x, gain):
    ms = jnp.mean(x * x, axis=-1, keepdims=True)
    return x * lax.rsqrt(ms + NORM_EPS) * gain


def _sigmoid(z):
    return 1.0 / (1.0 + jnp.exp(-z))


def _const_spec(shape):
    nd = len(shape)
    return pl.BlockSpec(shape, lambda *_: (0,) * nd, pipeline_mode=pl.Buffered(1))


def _params(sem, flags=None):
    return pltpu.CompilerParams(dimension_semantics=sem, vmem_limit_bytes=VMEM_LIMIT_BYTES, flags=flags)


def _in_proj_kernel(x_ref, pos_ref, gn_ref, wq_ref, wkT_ref, wv_ref, wg_ref, ws_ref, wa2T_ref, ba2_ref, gan_ref,
                    invf_ref, qn_ref, wuqT_ref, kvn_ref, wuk_ref, wuvT_ref,
                    gla_ref, qT_ref, k_ref, vT_ref, state_ref, *, tm, tk, tiles_per_q):
    @pl.when(pl.program_id(1) == 0)
    def _():
        state_ref[...] = jnp.zeros_like(state_ref)

    gla = {}
    gla_steps = _gla_phases(gla, gan_ref[...], state_ref, gla_ref, tm)

    h = _rms(x_ref[...], gn_ref[...]).astype(BF16)
    small = _dot(h, ws_ref[...])
    gla["q"] = (_dot(h, wq_ref[...]) * (GLA_DK ** -0.5)).astype(BF16)
    gla["kT"] = _dot_nt(wkT_ref[...], h).astype(BF16)
    gla["v"] = _dot(h, wv_ref[...]).astype(BF16)
    gla["g"] = _dot(h, wg_ref[...]).astype(BF16)
    last = small[:, KPE_OFF:]
    zT = _dot_nt(wa2T_ref[...], last.astype(BF16)) + ba2_ref[...]
    log_sig = jnp.minimum(zT, 0.0) - jnp.log(1.0 + jnp.exp(-jnp.abs(zT)))
    gla["laT"] = log_sig * (1.0 / GLA_GATE_TEMP)

    cq = _rms(small[:, :MLA_Q_LORA], qn_ref[...]).astype(BF16)
    ckv = _rms(small[:, MLA_Q_LORA:KPE_OFF], kvn_ref[...]).astype(BF16)
    qT = _dot_nt(wuqT_ref[...], cq)
    k_nope = _dot(ckv, wuk_ref[...]).astype(BF16)
    vT = _dot_nt(wuvT_ref[...], ckv).astype(BF16)

    half = MLA_ROPE // 2
    ang = invf_ref[...] * pos_ref[...].astype(F32)
    cos = jnp.cos(ang)
    sin = jnp.sin(ang)

    scale = MLA_QK ** -0.5 * LOG2E
    e_row = lax.broadcasted_iota(jnp.int32, (K_AUG - MLA_QK, tm), 0)
    q_pos = lax.broadcasted_iota(jnp.int32, (K_AUG - MLA_QK, tm), 1) + pl.program_id(1) * tm
    q_chunk = (q_pos % (tiles_per_q * tk)) // CHUNK
    bias = jnp.where(e_row > q_chunk, MASK_BIAS, 0.0).astype(BF16)
    for hd in range(MLA_HEADS):
        b0 = hd * MLA_QK
        o0 = hd * K_AUG
        qT_ref[o0:o0 + MLA_NOPE, :] = (qT[b0:b0 + MLA_NOPE] * scale).astype(BF16)
        x1 = qT[b0 + MLA_NOPE:b0 + MLA_NOPE + half]
        x2 = qT[b0 + MLA_NOPE + half:b0 + MLA_QK]
        qT_ref[o0 + MLA_NOPE:o0 + MLA_NOPE + half, :] = ((x1 * cos - x2 * sin) * scale).astype(BF16)
        qT_ref[o0 + MLA_NOPE + half:o0 + MLA_QK, :] = ((x2 * cos + x1 * sin) * scale).astype(BF16)
        qT_ref[o0 + MLA_QK:o0 + K_AUG, :] = bias

    zeros = jnp.zeros((LANES - MLA_ROPE, tm), F32)
    c_tab = jnp.concatenate([cos, cos, zeros], axis=0).T
    s_tab = jnp.concatenate([-sin, sin, zeros], axis=0).T
    lane = lax.broadcasted_iota(jnp.int32, last.shape, 1)
    swapped = jnp.where(lane < half, pltpu.roll(last, LANES - half, axis=1), pltpu.roll(last, half, axis=1))
    kpe = (last * c_tab + swapped * s_tab)[:, :MLA_ROPE].astype(BF16)

    ones = jnp.ones((V_AUG - MLA_V, tk), BF16)
    e = lax.broadcasted_iota(jnp.int32, (tk, K_AUG - MLA_QK), 1)
    chunk = lax.broadcasted_iota(jnp.int32, (tk, K_AUG - MLA_QK), 0) // CHUNK
    for s in range(tm // tk):
        rows = slice(s * tk, (s + 1) * tk)
        key_tile = pl.program_id(1) * (tm // tk) + s
        onehot = (e == chunk + (key_tile % tiles_per_q) * (tk // CHUNK)).astype(BF16)
        for hd in range(MLA_HEADS):
            k_ref[hd, s, :, :MLA_NOPE] = k_nope[rows, hd * MLA_NOPE:(hd + 1) * MLA_NOPE]
            k_ref[hd, s, :, MLA_NOPE:MLA_QK] = kpe[rows]
            k_ref[hd, s, :, MLA_QK:] = onehot
            vT_ref[hd, s, :MLA_V, :] = vT[hd * MLA_V:(hd + 1) * MLA_V, rows]
            vT_ref[hd, s, MLA_V:, :] = ones

    for _ in gla_steps:
        pass


def _in_proj(x2, pos3, gn, wq, wkT, wv, wg, ws, wa2T, ba2, gan, invf, qn, wuqT, kvn, wuk, wuvT, B, S, tm, tk,
             tiles_per_q):
    T, D = x2.shape
    nt = S // tm
    sub = tm // tk
    consts = (gn, wq, wkT, wv, wg, ws, wa2T, ba2, gan, invf, qn, wuqT, kvn, wuk, wuvT)
    return pl.pallas_call(
        functools.partial(_in_proj_kernel, tm=tm, tk=tk, tiles_per_q=tiles_per_q),
        grid=(B, nt),
        in_specs=[pl.BlockSpec((tm, D), lambda b, t: (b * nt + t, 0)), pl.BlockSpec((None, 1, tm), lambda b, t: (b, 0, t))]
        + [_const_spec(c.shape) for c in consts],
        out_specs=[
            pl.BlockSpec((tm, GLA_VW), lambda b, t: (b * nt + t, 0)),
            pl.BlockSpec((None, MLA_HEADS * K_AUG, tm), lambda b, t: (b, 0, t)),
            pl.BlockSpec((None, MLA_HEADS, sub, tk, K_AUG), lambda b, t: (b, 0, t, 0, 0)),
            pl.BlockSpec((None, MLA_HEADS, sub, V_AUG, tk), lambda b, t: (b, 0, t, 0, 0)),
        ],
        out_shape=[
            jax.ShapeDtypeStruct((T, GLA_VW), BF16),
            jax.ShapeDtypeStruct((B, MLA_HEADS * K_AUG, S), BF16),
            jax.ShapeDtypeStruct((B, MLA_HEADS, S // tk, tk, K_AUG), BF16),
            jax.ShapeDtypeStruct((B, MLA_HEADS, S // tk, V_AUG, tk), BF16),
        ],
        scratch_shapes=[pltpu.VMEM((GLA_HEADS, GLA_DK, GLA_DV), F32)],
        compiler_params=_params(("parallel", "arbitrary")),
        name="in_proj",
    )(x2, pos3, *consts)


PAIR = 2 * CHUNK


def _gla_phases(env, gn, state_ref, o_ref, ts):
    row = lax.broadcasted_iota(jnp.int32, (PAIR, PAIR), 0)
    col = lax.broadcasted_iota(jnp.int32, (PAIR, PAIR), 1)
    later = ((row // CHUNK == col // CHUNK) & (row > col)).astype(BF16)
    lane = lax.broadcasted_iota(jnp.int32, (GLA_DK, PAIR), 1)
    first = lane < CHUNK

    units = [(p, h) for p in range(ts // PAIR) for h in range(GLA_HEADS)]
    tok = lambda p: slice(p * PAIR, (p + 1) * PAIR)
    ks = lambda h: slice(h * GLA_DK, (h + 1) * GLA_DK)
    vs = lambda h: slice(h * GLA_DV, (h + 1) * GLA_DV)

    to_end = {}
    for p, h in units:
        la = env["laT"][ks(h), tok(p)]
        la_hi = la.astype(BF16)
        la_lo = (la - la_hi.astype(F32)).astype(BF16)
        to_end[p, h] = _dot(la_hi, later) + _dot(la_lo, later)
    yield

    kv, dec = {}, {}
    for p, h in units:
        la = env["laT"][ks(h), tok(p)]
        kd = env["kT"][ks(h), tok(p)].astype(F32) * jnp.exp(to_end[p, h])
        v = env["v"][tok(p), vs(h)]
        kv[p, 0, h] = _dot(jnp.where(first, kd, 0.0).astype(BF16), v)
        kv[p, 1, h] = _dot(jnp.where(first, 0.0, kd).astype(BF16), v)
        dec[p, 0, h] = jnp.exp(jnp.sum(jnp.where(first, la, 0.0), axis=1, keepdims=True))
        dec[p, 1, h] = jnp.exp(jnp.sum(jnp.where(first, 0.0, la), axis=1, keepdims=True))
    yield

    states = [state_ref[h] for h in range(GLA_HEADS)]
    out = {}
    for p in range(ts // PAIR):
        for c in range(2):
            rows = slice(p * PAIR + c * CHUNK, p * PAIR + (c + 1) * CHUNK)
            for h in range(GLA_HEADS):
                states[h] = dec[p, c, h] * states[h] + kv[p, c, h]
                out[p, c, h] = _dot(env["q"][rows, ks(h)], states[h].astype(BF16))
    for h in range(GLA_HEADS):
        state_ref[h] = states[h]
    yield

    for p, h in units:
        o = _rms(jnp.concatenate([out[p, 0, h], out[p, 1, h]], axis=0), gn)
        g = env["g"][tok(p), vs(h)].astype(F32)
        o_ref[tok(p), vs(h)] = (o * (g * _sigmoid(g))).astype(BF16)
    yield


def _colmax(x):
    while x.shape[0] > 8 and x.shape[0] % 16 == 0:
        half = x.shape[0] // 2
        x = jnp.maximum(x[:half], x[half:])
    return jnp.max(x, axis=0, keepdims=True)


ATTN_SCRATCH_PER_HEAD = 10
QGROUP = 256


def _mla_attn_kernel(q_ref, qnext_ref, k_ref, vT_ref, o_ref, *scratch, tq, tk, heads):
    i = pl.program_id(2)
    r = tq // tk
    n = i * r
    per_head = [scratch[hd * ATTN_SCRATCH_PER_HEAD:(hd + 1) * ATTN_SCRATCH_PER_HEAD] for hd in range(heads)]

    def scores(hd, j, queries_ref, cols, biased):
        q0 = hd * K_AUG
        if biased:
            return _dot(k_ref[hd, j], queries_ref[q0:q0 + K_AUG, cols])
        return _dot(k_ref[hd, j, :, :MLA_QK], queries_ref[q0:q0 + MLA_QK, cols])

    def first_tile_scores(queries_ref):
        for hd, (acc_ref, m_ref, s0_ref, s1_ref, x0_ref, x1_ref, p0_ref, p1_ref, a0_ref, a1_ref) in enumerate(per_head):
            sT = scores(hd, 0, queries_ref, slice(None), False)
            s0_ref[...] = sT
            x0_ref[...] = _colmax(sT)

    @pl.when(i == 0)
    def _():
        first_tile_scores(q_ref)

    for acc_ref, m_ref, s0_ref, s1_ref, x0_ref, x1_ref, p0_ref, p1_ref, a0_ref, a1_ref in per_head:
        acc_ref[...] = jnp.zeros_like(acc_ref)
        m_ref[...] = jnp.full_like(m_ref, -jnp.inf)
        p1_ref[...] = jnp.zeros_like(p1_ref)
        a1_ref[...] = jnp.ones_like(a1_ref)

    def stage(j, slot, biased_next, c_prev, c_cur, c_next):
        for c in range(tq // QGROUP):
            cols = slice(c * QGROUP, (c + 1) * QGROUP)
            for hd, (acc_ref, m_ref, s0_ref, s1_ref, x0_ref, x1_ref, p0_ref, p1_ref, a0_ref, a1_ref) in enumerate(per_head):
                s_refs, x_refs, p_refs, a_refs = (s0_ref, s1_ref), (x0_ref, x1_ref), (p0_ref, p1_ref), (a0_ref, a1_ref)
                s_cur, x_cur, p_cur, a_cur = s_refs[slot], x_refs[slot], p_refs[slot], a_refs[slot]
                s_nxt, x_nxt, p_prv, a_prv = s_refs[1 - slot], x_refs[1 - slot], p_refs[1 - slot], a_refs[1 - slot]
                if c_next is not None and c * QGROUP >= c_next:
                    sT = scores(hd, j + 1, q_ref, cols, biased_next)
                    s_nxt[:, cols] = sT
                    x_nxt[:, cols] = _colmax(sT)
                if c_prev is not None and c * QGROUP >= c_prev:
                    pv = _dot(vT_ref[hd, jnp.maximum(j - 1, 0)], p_prv[:, cols])
                    acc_ref[:, cols] = a_prv[:, cols] * acc_ref[:, cols] + pv
                if c_cur is not None and c * QGROUP >= c_cur:
                    m_old = m_ref[:, cols]
                    m_new = jnp.maximum(m_old, x_cur[:, cols])
                    a_cur[:, cols] = jnp.exp2(m_old - m_new)
                    p_cur[:, cols] = jnp.exp2(s_cur[:, cols] - m_new).astype(BF16)
                    m_ref[:, cols] = m_new

    def body(jj, carry):
        for u in range(r):
            stage(r * jj + u, u % 2, False, 0, 0, 0)
        return carry

    lax.fori_loop(0, i, body, 0)
    own = slice(0, tk)
    for hd, (acc_ref, m_ref, s0_ref, s1_ref, x0_ref, x1_ref, p0_ref, p1_ref, a0_ref, a1_ref) in enumerate(per_head):
        sT = scores(hd, n, q_ref, own, True)
        s0_ref[:, own] = sT
        x0_ref[:, own] = _colmax(sT)
    for d in range(r):
        stage(n + d, d % 2, True, max(d - 1, 0) * tk, d * tk, (d + 1) * tk if d + 1 < r else None)
        if d == r - 2:
            first_tile_scores(qnext_ref)
    stage(n + r, r % 2, True, (r - 1) * tk, None, None)
    for hd, refs in enumerate(per_head):
        acc_ref = refs[0]
        o_ref[:, hd * MLA_V:(hd + 1) * MLA_V] = (acc_ref[:MLA_V, :] / acc_ref[MLA_V:MLA_V + 1, :]).T.astype(o_ref.dtype)


def _mla_attn(qa, k5, vT5, B, S, tq, heads):
    nk, tk = k5.shape[2], k5.shape[3]
    nq = S // tq
    assert (tq // tk) % 2 == 0 and tk % QGROUP == 0 and tq // CHUNK <= K_AUG - MLA_QK and MLA_HEADS % heads == 0
    per_head = [
        pltpu.VMEM((V_AUG, tq), F32), pltpu.VMEM((1, tq), F32),
        pltpu.VMEM((tk, tq), F32), pltpu.VMEM((tk, tq), F32),
        pltpu.VMEM((1, tq), F32), pltpu.VMEM((1, tq), F32),
        pltpu.VMEM((tk, tq), BF16), pltpu.VMEM((tk, tq), BF16),
        pltpu.VMEM((1, tq), F32), pltpu.VMEM((1, tq), F32),
    ]
    assert len(per_head) == ATTN_SCRATCH_PER_HEAD
    return pl.pallas_call(
        functools.partial(_mla_attn_kernel, tq=tq, tk=tk, heads=heads),
        grid=(B, MLA_HEADS // heads, nq),
        in_specs=[
            pl.BlockSpec((None, heads * K_AUG, tq), lambda b, h, i: (b, h, i)),
            pl.BlockSpec((None, heads * K_AUG, tq), lambda b, h, i: (b, h, jnp.minimum(i + 1, nq - 1))),
            pl.BlockSpec((None, heads, nk, tk, K_AUG), lambda b, h, i: (b, h, 0, 0, 0)),
            pl.BlockSpec((None, heads, nk, V_AUG, tk), lambda b, h, i: (b, h, 0, 0, 0)),
        ],
        out_specs=pl.BlockSpec((None, tq, heads * MLA_V), lambda b, h, i: (b, i, h)),
        out_shape=jax.ShapeDtypeStruct((B, S, MLA_W), BF16),
        scratch_shapes=per_head * heads,
        compiler_params=_params(("arbitrary", "arbitrary", "arbitrary")),
        name="mla_attn",
    )(qa, qa, k5, vT5)


def _mix_kernel(x_ref, a_ref, b_ref, gn_ref, wgate_ref, bgate_ref, woa_ref, wob_ref, wout_ref, pn_ref, o_ref):
    x = x_ref[...]
    d = x.shape[-1]
    h = _rms(x, gn_ref[...]).astype(BF16)
    gates = _sigmoid(_dot(h, wgate_ref[...]) + bgate_ref[...])
    y_a = _dot(a_ref[...], woa_ref[...])
    y_b = _dot(b_ref[...], wob_ref[...])
    mixed = (gates[:, :d] * y_a + gates[:, d:] * y_b).astype(BF16)
    o_ref[...] = x + _rms(_dot(mixed, wout_ref[...]), pn_ref[...])


def _mix(x2, a, b, gn, wgate, bgate, woa, wob, wout, pn, tm):
    T, D = x2.shape
    row = lambda i: (i, 0)
    return pl.pallas_call(
        _mix_kernel,
        grid=(T // tm,),
        in_specs=[
            pl.BlockSpec((tm, D), row), pl.BlockSpec((tm, a.shape[1]), row), pl.BlockSpec((tm, b.shape[1]), row),
            _const_spec(gn.shape), _const_spec(wgate.shape), _const_spec(bgate.shape),
            _const_spec(woa.shape), _const_spec(wob.shape), _const_spec(wout.shape), _const_spec(pn.shape),
        ],
        out_specs=pl.BlockSpec((tm, D), row),
        out_shape=jax.ShapeDtypeStruct((T, D), F32),
        compiler_params=_params(("parallel",)),
        name="mix",
    )(x2, a, b, gn, wgate, bgate, woa, wob, wout, pn)


def _ffn_kernel(x_ref, gn_ref, wg_ref, wu_ref, wd_ref, pn_ref, o_ref):
    x = x_ref[...]
    h = _rms(x, gn_ref[...]).astype(BF16)
    gate = _dot(h, wg_ref[...])
    up = _dot(h, wu_ref[...])
    act = (gate * _sigmoid(gate) * up).astype(BF16)
    o_ref[...] = x + _rms(_dot(act, wd_ref[...]), pn_ref[...])


def _ffn(x2, gn, wg, wu, wd, pn, tm):
    T, D = x2.shape
    row = lambda i: (i, 0)
    return pl.pallas_call(
        _ffn_kernel,
        grid=(T // tm,),
        in_specs=[
            pl.BlockSpec((tm, D), row),
            _const_spec(gn.shape), _const_spec(wg.shape), _const_spec(wu.shape),
            _const_spec(wd.shape), _const_spec(pn.shape),
        ],
        out_specs=pl.BlockSpec((tm, D), row),
        out_shape=jax.ShapeDtypeStruct((T, D), F32),
        compiler_params=_params(("parallel",)),
        name="ffn",
    )(x2, gn, wg, wu, wd, pn)


def _tile(n, pref):
    t = min(n, pref)
    assert n % t == 0, (n, t)
    return t


def _layer(x2, pos3, B, S, pre_mix_norm, w_in, w_a2, b_a2, gla_norm, w_o_gla, q_norm, w_uq, kv_norm, w_ukv,
           w_o_mla, w_gate, b_gate, w_out, post_mix_norm, pre_ffn_norm, w_ffn_gate, w_ffn_up, w_ffn_down,
           post_ffn_norm):
    T, D = x2.shape
    bf = lambda w: w.astype(BF16)
    r2 = lambda v: v.reshape(1, -1)

    o = 0
    wq = bf(w_in[:, o:o + GLA_KW]); o += GLA_KW
    wkT = bf(w_in[:, o:o + GLA_KW]).T; o += GLA_KW
    wv = bf(w_in[:, o:o + GLA_VW]); o += GLA_VW
    wg = bf(w_in[:, o:o + GLA_VW]); o += GLA_VW
    w_ha = w_in[:, o:o + GLA_LOWRANK]; o += GLA_LOWRANK
    w_mla = w_in[:, o:]
    pad = SMALL_W - HA_OFF - GLA_LOWRANK
    ws = bf(jnp.concatenate([w_mla, w_ha, jnp.zeros((D, pad), w_in.dtype)], axis=1))
    wa2T = jnp.zeros((GLA_KW, LANES), F32).at[:, MLA_ROPE:MLA_ROPE + GLA_LOWRANK].set(w_a2.T)
    wa2T = bf(wa2T)
    ba2 = b_a2.reshape(-1, 1)

    wuqT = bf(w_uq).T
    w_ukv3 = w_ukv.reshape(MLA_KV_LORA, MLA_HEADS, MLA_NOPE + MLA_V)
    wuk = bf(w_ukv3[:, :, :MLA_NOPE].reshape(MLA_KV_LORA, MLA_HEADS * MLA_NOPE))
    wuvT = bf(w_ukv3[:, :, MLA_NOPE:].reshape(MLA_KV_LORA, MLA_W)).T
    invf = (1.0 / (ROPE_THETA ** (jnp.arange(0, MLA_ROPE, 2, dtype=F32) / MLA_ROPE))).reshape(-1, 1)

    tm = _tile(T, 1024)
    tk = _tile(S, 256)
    tq = _tile(S, 2048)
    gla_o, qT, k5, vT5 = _in_proj(
        x2, pos3, r2(pre_mix_norm), wq, wkT, wv, wg, ws, wa2T, ba2, r2(gla_norm), invf, r2(q_norm), wuqT, r2(kv_norm),
        wuk, wuvT, B, S, _tile(S, 512), tk, tq // tk)
    mla_o = _mla_attn(qT, k5, vT5, B, S, tq, 1).reshape(T, MLA_W)

    x1 = _mix(x2, gla_o, mla_o, r2(pre_mix_norm), bf(w_gate), r2(b_gate), bf(w_o_gla), bf(w_o_mla), bf(w_out),
              r2(post_mix_norm), tm)
    return _ffn(x1, r2(pre_ffn_norm), bf(w_ffn_gate), bf(w_ffn_up), bf(w_ffn_down), r2(post_ffn_norm), tm)


def kernel(x, positions, pre_mix_norm, w_in, w_a2, b_a2, gla_norm, w_o_gla, q_norm, w_uq, kv_norm, w_ukv, w_o_mla,
           w_gate, b_gate, w_out, post_mix_norm, pre_ffn_norm, w_ffn_gate, w_ffn_up, w_ffn_down, post_ffn_norm):
    B, S, D = x.shape
    x2 = x.reshape(B * S, D)
    pos3 = positions.reshape(B, 1, S)
    per_layer = (pre_mix_norm, w_in, w_a2, b_a2, gla_norm, w_o_gla, q_norm, w_uq, kv_norm, w_ukv, w_o_mla, w_gate,
                 b_gate, w_out, post_mix_norm, pre_ffn_norm, w_ffn_gate, w_ffn_up, w_ffn_down, post_ffn_norm)
    for l in range(pre_mix_norm.shape[0]):
        x2 = _layer(x2, pos3, B, S, *[p[l] for p in per_layer])
    return x2.reshape(B, S, D)
```
